```python
import jax, jax.numpy as jnp
from jax import lax
import numpy as np

D_MODEL = 1024
BATCH = 4
SEQ = 8192
DEPTH = 2

CHUNK = 64
SB_HEADS = 8
SB_HEAD_DIM = D_MODEL // 16
SB_WIDTH = SB_HEADS * SB_HEAD_DIM
SB_BLOCK = 128
HG_HEAD_DIM = D_MODEL // 8
HG_HEADS = 4
HG_WIDTH = HG_HEADS * HG_HEAD_DIM
HG_CHUNK = CHUNK // 4
MIX_WIDTH = SB_WIDTH + HG_WIDTH
IN_WIDTH = 3 * SB_WIDTH + 4 * HG_WIDTH
SPLIT_POINTS = (SB_WIDTH, 2 * SB_WIDTH, 3 * SB_WIDTH,
                3 * SB_WIDTH + HG_WIDTH, 3 * SB_WIDTH + 2 * HG_WIDTH,
                3 * SB_WIDTH + 3 * HG_WIDTH)
CONV_WIDTH = 31
D_FF = 4 * D_MODEL
N_AB = (DEPTH + 1) // 2
N_CONV = DEPTH // 2
RMS_EPS = 1e-6
LN_EPS = 1e-5

kernel_name = "hybrid_stickbreak_hgrn2_conformer_trunk"


def rmsnorm(x, g):
    xf = x.astype(jnp.float32)
    y = xf * lax.rsqrt(jnp.mean(xf * xf, axis=-1, keepdims=True) + RMS_EPS)
    return (y * g).astype(x.dtype)


def stick_breaking_attention(q, k, v):
    b, s, h, dh = q.shape
    nb = s // SB_BLOCK
    scale = dh ** -0.5
    kh = k.transpose(0, 2, 1, 3)
    vh = v.transpose(0, 2, 1, 3)
    q_blocks = jnp.moveaxis(q.transpose(0, 2, 1, 3).reshape(b, h, nb, SB_BLOCK, dh), 2, 0)
    key_pos = jnp.arange(s)

    def one_block(args):
        qb, blk = args
        z = jnp.einsum('bhqd,bhkd->bhqk', qb, kh).astype(jnp.float32) * scale
        q_pos = blk * SB_BLOCK + jnp.arange(SB_BLOCK)
        mask = key_pos[None, :] < q_pos[:, None]
        log_beta = jax.nn.log_sigmoid(z)
        log_keep = jnp.where(mask, log_beta - z, 0.0)
        log_keep_after = lax.cumsum(log_keep, axis=3, reverse=True) - log_keep
        w = jnp.where(mask, jnp.exp(log_beta + log_keep_after), 0.0)
        return jnp.einsum('bhqk,bhkd->bhqd', w.astype(vh.dtype), vh)

    out = lax.map(one_block, (q_blocks, jnp.arange(nb)))
    out = jnp.moveaxis(out, 0, 2).reshape(b, h, s, dh).transpose(0, 2, 1, 3)
    return out.reshape(b, s, h * dh)


def hgrn2(q, f_raw, i, gate, lb, norm_g):
    b, s, h, d = q.shape
    n = s // HG_CHUNK
    f32 = jnp.float32
    f = lb + (1.0 - lb) * jax.nn.sigmoid(f_raw.astype(f32))
    g = jnp.log(f)
    kk = 1.0 - f

    def chunked(t):
        return t.transpose(0, 2, 1, 3).reshape(b, h, n, HG_CHUNK, d)

    qc, gc, kc, vc = chunked(q.astype(f32)), chunked(g), chunked(kk), chunked(i.astype(f32))
    G = jnp.cumsum(gc, axis=3)
    G_last = G[:, :, :, -1:, :]
    q_dec = qc * jnp.exp(G)
    k_intra = kc * jnp.exp(-G)
    k_state = kc * jnp.exp(G_last - G)
    decay = jnp.exp(G_last[:, :, :, 0, :])
    causal = jnp.tril(jnp.ones((HG_CHUNK, HG_CHUNK), dtype=bool))
    scores = jnp.where(causal, jnp.einsum('bhncd,bhnsd->bhncs', q_dec, k_intra), 0.0)
    o_intra = jnp.einsum('bhncs,bhnsv->bhncv', scores, vc)

    def step(state, xs):
        qd, ks, v, dec = xs
        o = jnp.einsum('bhcd,bhdv->bhcv', qd, state)
        state = state * dec[..., None] + jnp.einsum('bhcd,bhcv->bhdv', ks, v)
        return state, o

    xs = (jnp.moveaxis(q_dec, 2, 0), jnp.moveaxis(k_state, 2, 0),
          jnp.moveaxis(vc, 2, 0), jnp.moveaxis(decay, 2, 0))
    _, o_inter = lax.scan(step, jnp.zeros((b, h, d, d), f32), xs)
    o = o_intra + jnp.moveaxis(o_inter, 0, 2)
    o = o.reshape(b, h, s, d).transpose(0, 2, 1, 3)
    o = o * lax.rsqrt(jnp.mean(o * o, axis=-1, keepdims=True) + RMS_EPS) * norm_g
    o = o * jax.nn.silu(gate.astype(f32))
    return o.reshape(b, s, h * d).astype(q.dtype)


def parallel_ab_mixer(u, w_in, w_out, lb, hg_norm_g):
    b, s, _ = u.shape
    proj = u @ w_in
    sb_q, sb_k, sb_v, hg_q, hg_f, hg_i, hg_g = jnp.split(proj, SPLIT_POINTS, axis=-1)
    sb_shape = (b, s, SB_HEADS, SB_HEAD_DIM)
    hg_shape = (b, s, HG_HEADS, HG_HEAD_DIM)
    o_sb = stick_breaking_attention(sb_q.reshape(sb_shape), sb_k.reshape(sb_shape), sb_v.reshape(sb_shape))
    o_hg = hgrn2(hg_q.reshape(hg_shape), hg_f.reshape(hg_shape), hg_i.reshape(hg_shape),
                 hg_g.reshape(hg_shape), lb.reshape(HG_HEADS, HG_HEAD_DIM), hg_norm_g)
    return jnp.concatenate([o_sb, o_hg], axis=-1) @ w_out


def conformer_conv(u, w_glu, b_glu, w_dw, b_dw, ln_g, ln_b, w_pw, b_pw):
    a = jax.nn.glu(u @ w_glu + b_glu, axis=-1)
    y = lax.conv_general_dilated(a, w_dw[:, None, :], window_strides=(1,),
                                 padding=[(CONV_WIDTH - 1, 0)],
                                 dimension_numbers=('NWC', 'WIO', 'NWC'),
                                 feature_group_count=D_MODEL) + b_dw
    yf = y.astype(jnp.float32)
    mu = jnp.mean(yf, axis=-1, keepdims=True)
    var = jnp.mean(jnp.square(yf - mu), axis=-1, keepdims=True)
    y = jax.nn.silu((yf - mu) * lax.rsqrt(var + LN_EPS) * ln_g + ln_b).astype(u.dtype)
    return y @ w_pw + b_pw


def squared_relu_mlp(u, w1, w2):
    return jnp.square(jax.nn.relu(u @ w1)) @ w2


def setup_inputs(seed: int = 0) -> dict:
    key = jax.random.key(seed)
    ks = jax.random.split(key, 20)
    nrm = jax.random.normal
    f32 = jnp.float32
    return {
        "x": nrm(ks[0], (BATCH, SEQ, D_MODEL), f32),
        "norm_mix_g": 1.0 + 0.02 * nrm(ks[1], (DEPTH, D_MODEL), f32),
        "norm_ffn_g": 1.0 + 0.02 * nrm(ks[2], (DEPTH, D_MODEL), f32),
        "w_in_ab": nrm(ks[3], (N_AB, D_MODEL, IN_WIDTH), f32) * D_MODEL ** -0.5,
        "w_out_ab": nrm(ks[4], (N_AB, MIX_WIDTH, D_MODEL), f32) * MIX_WIDTH ** -0.5,
        "hg_lb_logits": 0.1 * nrm(ks[5], (DEPTH + 1, HG_WIDTH), f32),
        "hg_norm_g": 1.0 + 0.02 * nrm(ks[6], (N_AB, HG_HEADS, HG_HEAD_DIM), f32),
        "conv_w_glu": nrm(ks[7], (N_CONV, D_MODEL, 2 * D_MODEL), f32) * D_MODEL ** -0.5,
        "conv_b_glu": 0.02 * nrm(ks[8], (N_CONV, 2 * D_MODEL), f32),
        "conv_w_dw": nrm(ks[9], (N_CONV, CONV_WIDTH, D_MODEL), f32) * CONV_WIDTH ** -0.5,
        "conv_b_dw": 0.02 * nrm(ks[10], (N_CONV, D_MODEL), f32),
        "conv_ln_g": 1.0 + 0.02 * nrm(ks[11], (N_CONV, D_MODEL), f32),
        "conv_ln_b": 0.02 * nrm(ks[12], (N_CONV, D_MODEL), f32),
        "conv_w_pw": nrm(ks[13], (N_CONV, D_MODEL, D_MODEL), f32) * D_MODEL ** -0.5,
        "conv_b_pw": 0.02 * nrm(ks[14], (N_CONV, D_MODEL), f32),
        "w_ff1": nrm(ks[15], (DEPTH, D_MODEL, D_FF), f32) * D_MODEL ** -0.5,
        "w_ff2": nrm(ks[16], (DEPTH, D_FF, D_MODEL), f32) * D_FF ** -0.5,
        "final_norm_g": 1.0 + 0.02 * nrm(ks[17], (D_MODEL,), f32),
    }


def reference(x, norm_mix_g, norm_ffn_g, w_in_ab, w_out_ab, hg_lb_logits, hg_norm_g,
              conv_w_glu, conv_b_glu, conv_w_dw, conv_b_dw, conv_ln_g, conv_ln_b,
              conv_w_pw, conv_b_pw, w_ff1, w_ff2, final_norm_g):
    lower_bounds = jnp.cumsum(jax.nn.softmax(hg_lb_logits.astype(jnp.float32), axis=0), axis=0)
    h = x
    for layer in range(DEPTH):
        j = layer // 2
        u = rmsnorm(h, norm_mix_g[layer])
        if layer % 2 == 0:
            h = h + parallel_ab_mixer(u, w_in_ab[j], w_out_ab[j], lower_bounds[layer], hg_norm_g[j])
        else:
            h = h + conformer_conv(u, conv_w_glu[j], conv_b_glu[j], conv_w_dw[j], conv_b_dw[j],
                                   conv_ln_g[j], conv_ln_b[j], conv_w_pw[j], conv_b_pw[j])
        u = rmsnorm(h, norm_ffn_g[layer])
        h = h + squared_relu_mlp(u, w_ff1[layer], w_ff2[layer])
    return rmsnorm(h, final_norm_g)
```

```python
import functools

import jax
import jax.numpy as jnp
from jax import lax
from jax.experimental import pallas as pl
from jax.experimental.pallas import tpu as pltpu

F32 = jnp.float32
BF16 = jnp.bfloat16

RMS_EPS = 1e-6
LN_EPS = 1e-5

SB_HEADS = 8
SB_HEAD_DIM = 64
SB_WIDTH = SB_HEADS * SB_HEAD_DIM
HG_HEADS = 4
HG_HEAD_DIM = 128
HG_WIDTH = HG_HEADS * HG_HEAD_DIM
CONV_WIDTH = 31

LANES = 128
SB_BLOCK = 128
SB_UNDERFLOW = -92.0
HG_STEP = 32
CONV_HIST = 32
VMEM_LIMIT = 56 * 1024 * 1024

_NT = (((1,), (1,)), ((), ()))
_TN = (((0,), (0,)), ((), ()))


def _rms(x, g):
    return x * lax.rsqrt(jnp.mean(x * x, axis=-1, keepdims=True) + RMS_EPS) * g


def _dot(a, b):
    return jnp.dot(a, b, preferred_element_type=F32)


def _const_spec(shape):
    nd = len(shape)
    return pl.BlockSpec(shape, lambda *_: (0,) * nd, pipeline_mode=pl.Buffered(1))


def _inproj_kernel(x_ref, g_ref, w_ref, o_ref, *, tn):
    u = _rms(x_ref[...], g_ref[...]).astype(BF16)
    for n in range(w_ref.shape[1] // tn):
        cols = slice(n * tn, (n + 1) * tn)
        o_ref[:, cols] = _dot(u, w_ref[:, cols]).astype(BF16)


def _inproj(x2d, g, w, *, tm=512, tn=512):
    t, d = x2d.shape
    n = w.shape[1]
    tm = min(tm, t)
    return pl.pallas_call(
        functools.partial(_inproj_kernel, tn=tn),
        grid=(t // tm,),
        in_specs=[pl.BlockSpec((tm, d), lambda i: (i, 0)),
                  _const_spec((1, d)),
                  _const_spec((d, n))],
        out_specs=pl.BlockSpec((tm, n), lambda i: (i, 0)),
        out_shape=jax.ShapeDtypeStruct((t, n), BF16),
        compiler_params=pltpu.CompilerParams(dimension_semantics=("arbitrary",),
                                             vmem_limit_bytes=VMEM_LIMIT),
        name="inproj",
    )(x2d, g, w)


def _sb_kernel(q_ref, k_ref, v_ref, o_ref, r_ref, acc_ref):
    qi = pl.program_id(2)
    blk = SB_BLOCK
    row = lax.broadcasted_iota(jnp.int32, (blk, blk), 0)
    col = lax.broadcasted_iota(jnp.int32, (blk, blk), 1)
    lo = col < SB_HEAD_DIM
    causal = col < row
    trow = lax.broadcasted_iota(jnp.int32, (blk, 2 * blk), 0)
    tcol = lax.broadcasted_iota(jnp.int32, (blk, 2 * blk), 1)
    tri_ones = jnp.where((trow > tcol) | (tcol >= blk), 1.0, 0.0).astype(BF16)

    q = q_ref[0]
    zero = jnp.zeros_like(q)
    q_heads = (jnp.where(lo, q, zero), jnp.where(lo, zero, q))

    def block(kb, masked):
        rows = pl.ds(pl.multiple_of(kb * blk, blk), blk)
        k = k_ref[0, rows, :]
        v = v_ref[0, rows, :]
        v_heads = (jnp.where(lo, v, zero), jnp.where(lo, zero, v))
        pv = None
        r_max = None
        for hh in range(2):
            z = lax.dot_general(q_heads[hh], k, _NT, preferred_element_type=F32)
            sp = jnp.log(1.0 + jnp.exp(-jnp.abs(z)))
            log_beta = jnp.minimum(z, 0.0) - sp
            log_keep = log_beta - z
            if masked:
                log_keep = jnp.where(causal, log_keep, 0.0)
            sums = _dot(log_keep.astype(BF16), tri_ones)
            r_old = r_ref[hh]
            w = jnp.exp(log_beta + sums[:, :blk] + r_old)
            if masked:
                w = jnp.where(causal, w, 0.0)
            r_new = r_old + sums[:, blk:]
            r_ref[hh] = r_new
            part = _dot(w.astype(BF16), v_heads[hh])
            pv = part if pv is None else pv + part
            m = jnp.max(r_new)
            r_max = m if r_max is None else jnp.maximum(r_max, m)
        return pv, r_max

    r_ref[...] = jnp.zeros_like(r_ref)
    pv0, m0 = block(qi, True)
    acc_ref[...] = pv0

    def cond(c):
        kb, m = c
        return jnp.logical_and(kb >= 0, m > SB_UNDERFLOW)

    def body(c):
        kb, _ = c
        pv, m = block(kb, False)
        acc_ref[...] += pv
        return kb - 1, m

    lax.while_loop(cond, body, (qi - 1, m0))
    o_ref[0] = acc_ref[...].astype(BF16)


def _sb_attention(proj3):
    b, s, _ = proj3.shape
    pairs = SB_WIDTH // LANES
    nq = s // SB_BLOCK
    return pl.pallas_call(
        _sb_kernel,
        grid=(b, pairs, nq),
        in_specs=[pl.BlockSpec((1, SB_BLOCK, LANES), lambda bi, p, qi: (bi, qi, p)),
                  pl.BlockSpec((1, s, LANES), lambda bi, p, qi: (bi, 0, pairs + p)),
                  pl.BlockSpec((1, s, LANES), lambda bi, p, qi: (bi, 0, 2 * pairs + p))],
        out_specs=pl.BlockSpec((1, SB_BLOCK, LANES), lambda bi, p, qi: (bi, qi, p)),
        out_shape=jax.ShapeDtypeStruct((b, s, SB_WIDTH), BF16),
        scratch_shapes=[pltpu.VMEM((2, SB_BLOCK, SB_BLOCK), F32),
                        pltpu.VMEM((SB_BLOCK, LANES), F32)],
        compiler_params=pltpu.CompilerParams(
            dimension_semantics=("arbitrary", "arbitrary", "arbitrary"),
            vmem_limit_bytes=VMEM_LIMIT),
        name="sb_attention",
    )(proj3, proj3, proj3)


def _hg_kernel(q_ref, f_ref, i_ref, gate_ref, lbl_ref, ng_ref, o_ref, state_ref, *, layer):
    si = pl.program_id(2)
    step = HG_STEP
    half = step // 2

    @pl.when(si == 0)
    def _():
        state_ref[...] = jnp.zeros_like(state_ref)

    logits = lbl_ref[...]
    e = jnp.exp(logits - jnp.max(logits, axis=0, keepdims=True))
    lb = jnp.sum(e[:layer + 1], axis=0, keepdims=True) / jnp.sum(e, axis=0, keepdims=True)
    ng = ng_ref[0]

    r = lax.broadcasted_iota(jnp.int32, (step, step), 0)
    c = lax.broadcasted_iota(jnp.int32, (step, step), 1)
    tril = r >= c
    tril_f = jnp.where(tril, 1.0, 0.0).astype(F32)

    def body(n, carry):
        rows = pl.ds(pl.multiple_of(n * step, step), step)
        q = q_ref[0, rows, :].astype(F32)
        f = lb + (1.0 - lb) * jax.nn.sigmoid(f_ref[0, rows, :].astype(F32))
        v = i_ref[0, rows, :]
        g = jnp.log(f)
        kk = 1.0 - f
        big_g = jnp.dot(tril_f, g, preferred_element_type=F32,
                        precision=lax.Precision.HIGHEST)
        g_mid = big_g[half - 1:half]
        g_last = big_g[step - 1:step]
        q_intra = (q * jnp.exp(big_g - g_mid)).astype(BF16)
        k_intra = (kk * jnp.exp(g_mid - big_g)).astype(BF16)
        k_state = (kk * jnp.exp(g_last - big_g)).astype(BF16)
        q_inter = (q * jnp.exp(big_g)).astype(BF16)
        decay = jnp.exp(g_last)
        scores = lax.dot_general(q_intra, k_intra, _NT, preferred_element_type=F32)
        scores = jnp.where(tril, scores, 0.0).astype(BF16)
        state_t = state_ref[...]
        o = _dot(scores, v) + lax.dot_general(q_inter, state_t.astype(BF16), _NT,
                                              preferred_element_type=F32)
        upd_t = lax.dot_general(v, k_state, _TN, preferred_element_type=F32)
        state_ref[...] = state_t * decay + upd_t
        o = o * lax.rsqrt(jnp.mean(o * o, axis=-1, keepdims=True) + RMS_EPS) * ng
        o = o * jax.nn.silu(gate_ref[0, rows, :].astype(F32))
        o_ref[0, rows, :] = o.astype(BF16)
        return carry

    lax.fori_loop(0, q_ref.shape[1] // step, body, 0)


def _hgrn2(proj3, lb_logits, norm_g, *, layer, ts=1024):
    b, s, _ = proj3.shape
    ts = min(ts, s)
    base = 3 * SB_WIDTH // LANES
    depth1 = lb_logits.shape[0]

    def col(k):
        return pl.BlockSpec((1, ts, LANES), lambda bi, h, si: (bi, si, base + k * HG_HEADS + h))

    return pl.pallas_call(
        functools.partial(_hg_kernel, layer=layer),
        grid=(b, HG_HEADS, s // ts),
        in_specs=[col(0), col(1), col(2), col(3),
                  pl.BlockSpec((depth1, LANES), lambda bi, h, si: (0, h)),
                  pl.BlockSpec((1, 1, LANES), lambda bi, h, si: (h, 0, 0))],
        out_specs=pl.BlockSpec((1, ts, LANES), lambda bi, h, si: (bi, si, h)),
        out_shape=jax.ShapeDtypeStruct((b, s, HG_WIDTH), BF16),
        scratch_shapes=[pltpu.VMEM((HG_HEAD_DIM, HG_HEAD_DIM), F32)],
        compiler_params=pltpu.CompilerParams(
            dimension_semantics=("arbitrary", "arbitrary", "arbitrary"),
            vmem_limit_bytes=VMEM_LIMIT),
        name="hgrn2",
    )(proj3, proj3, proj3, proj3, lb_logits, norm_g.reshape(HG_HEADS, 1, HG_HEAD_DIM))


def _proj_mlp_kernel(ma_ref, mb_ref, h_ref, wp_ref, bp_ref, g_ref, w1_ref, w2_ref, gf_ref,
                     o_ref, *, tf, final_norm):
    half = ma_ref.shape[1]
    h1 = (h_ref[...] + _dot(ma_ref[...], wp_ref[:half, :]) + _dot(mb_ref[...], wp_ref[half:, :])
          + bp_ref[...])
    u = _rms(h1, g_ref[...]).astype(BF16)
    acc = h1
    for c in range(w1_ref.shape[1] // tf):
        a = jnp.maximum(_dot(u, w1_ref[:, c * tf:(c + 1) * tf]), 0.0)
        acc = acc + _dot((a * a).astype(BF16), w2_ref[c * tf:(c + 1) * tf, :])
    if final_norm:
        acc = _rms(acc, gf_ref[...])
    o_ref[...] = acc


def _proj_mlp(ma, mb, cols_a, cols_b, h2d, wp, bp, g, w1, w2, gf, *, final_norm, tm=512, tf=1024):
    t, d = h2d.shape
    half = d // 2
    dff = w1.shape[1]
    tm = min(tm, t)
    return pl.pallas_call(
        functools.partial(_proj_mlp_kernel, tf=tf, final_norm=final_norm),
        grid=(t // tm,),
        in_specs=[pl.BlockSpec((tm, half), lambda i: (i, cols_a)),
                  pl.BlockSpec((tm, half), lambda i: (i, cols_b)),
                  pl.BlockSpec((tm, d), lambda i: (i, 0)),
                  _const_spec((d, d)), _const_spec((1, d)), _const_spec((1, d)),
                  _const_spec((d, dff)), _const_spec((dff, d)), _const_spec((1, d))],
        out_specs=pl.BlockSpec((tm, d), lambda i: (i, 0)),
        out_shape=jax.ShapeDtypeStruct((t, d), F32),
        compiler_params=pltpu.CompilerParams(dimension_semantics=("arbitrary",),
                                             vmem_limit_bytes=VMEM_LIMIT),
        name="proj_mlp_final" if final_norm else "proj_mlp",
    )(ma, mb, h2d, wp, bp, g, w1, w2, gf)


def _conv_kernel(h_ref, g_ref, wg_ref, bg_ref, wdw_ref, bdw_ref, lng_ref, lnb_ref, o_ref,
                 abuf_ref, y_ref, *, rc):
    si = pl.program_id(1)
    ts, d = h_ref.shape[1], h_ref.shape[2]
    hist = CONV_HIST

    @pl.when(si == 0)
    def _():
        abuf_ref[0:hist, :] = jnp.zeros((hist, d), F32)

    @pl.when(si > 0)
    def _():
        abuf_ref[0:hist, :] = abuf_ref[ts:ts + hist, :]

    u = _rms(h_ref[0], g_ref[...]).astype(BF16)
    lin_a = _dot(u, wg_ref[:, :d]) + bg_ref[:, :d]
    lin_b = _dot(u, wg_ref[:, d:]) + bg_ref[:, d:]
    abuf_ref[hist:, :] = lin_a * jax.nn.sigmoid(lin_b)

    first = hist - (CONV_WIDTH - 1)
    for r0 in range(0, ts, rc):
        for l0 in range(0, d, LANES):
            lanes = slice(l0, l0 + LANES)
            acc = jnp.broadcast_to(bdw_ref[:, lanes], (rc, LANES))
            for j in range(CONV_WIDTH):
                acc = acc + wdw_ref[j:j + 1, lanes] * abuf_ref[r0 + first + j:r0 + first + j + rc, lanes]
            y_ref[r0:r0 + rc, lanes] = acc

    y = y_ref[...]
    mu = jnp.mean(y, axis=-1, keepdims=True)
    yc = y - mu
    var = jnp.mean(yc * yc, axis=-1, keepdims=True)
    yn = yc * lax.rsqrt(var + LN_EPS) * lng_ref[...] + lnb_ref[...]
    o_ref[0] = (yn * jax.nn.sigmoid(yn)).astype(BF16)


def _conv_front(h3, g, wg, bg, wdw, bdw, lng, lnb, *, ts=256, rc=64):
    b, s, d = h3.shape
    ts = min(ts, s)
    return pl.pallas_call(
        functools.partial(_conv_kernel, rc=rc),
        grid=(b, s // ts),
        in_specs=[pl.BlockSpec((1, ts, d), lambda bi, si: (bi, si, 0)),
                  _const_spec((1, d)), _const_spec((d, 2 * d)), _const_spec((1, 2 * d)),
                  _const_spec((CONV_WIDTH, d)), _const_spec((1, d)), _const_spec((1, d)),
                  _const_spec((1, d))],
        out_specs=pl.BlockSpec((1, ts, d), lambda bi, si: (bi, si, 0)),
        out_shape=jax.ShapeDtypeStruct((b, s, d), BF16),
        scratch_shapes=[pltpu.VMEM((ts + CONV_HIST, d), F32), pltpu.VMEM((ts, d), F32)],
        compiler_params=pltpu.CompilerParams(dimension_semantics=("arbitrary", "arbitrary"),
                                             vmem_limit_bytes=VMEM_LIMIT),
        name="conv_front",
    )(h3, g, wg, bg, wdw, bdw, lng, lnb)


def kernel(x, norm_mix_g, norm_ffn_g, w_in_ab, w_out_ab, hg_lb_logits, hg_norm_g, conv_w_glu,
           conv_b_glu, conv_w_dw, conv_b_dw, conv_ln_g, conv_ln_b, conv_w_pw, conv_b_pw,
           w_ff1, w_ff2, final_norm_g):
    b, s, d = x.shape
    depth = norm_mix_g.shape[0]
    row = lambda a: a.reshape(1, -1).astype(F32)
    zeros_d = jnp.zeros((1, d), F32)
    h = x.reshape(b * s, d)
    for layer in range(depth):
        j = layer // 2
        last = layer == depth - 1
        if layer % 2 == 0:
            scale = jnp.where(jnp.arange(w_in_ab.shape[2]) < SB_WIDTH, SB_HEAD_DIM ** -0.5, 1.0)
            w_in = (w_in_ab[j] * scale).astype(BF16)
            proj = _inproj(h, row(norm_mix_g[layer]), w_in).reshape(b, s, -1)
            o_sb = _sb_attention(proj).reshape(b * s, SB_WIDTH)
            o_hg = _hgrn2(proj, hg_lb_logits.astype(F32), hg_norm_g[j].astype(F32),
                          layer=layer).reshape(b * s, HG_WIDTH)
            mix = (o_sb, o_hg, 0, 0)
            wp, bp = w_out_ab[j].astype(BF16), zeros_d
        else:
            y = _conv_front(h.reshape(b, s, d), row(norm_mix_g[layer]),
                            conv_w_glu[j].astype(BF16), row(conv_b_glu[j]),
                            conv_w_dw[j].astype(F32), row(conv_b_dw[j]), row(conv_ln_g[j]),
                            row(conv_ln_b[j])).reshape(b * s, d)
            mix = (y, y, 0, 1)
            wp, bp = conv_w_pw[j].astype(BF16), row(conv_b_pw[j])
        h = _proj_mlp(*mix, h, wp, bp, row(norm_ffn_g[layer]), w_ff1[layer].astype(BF16),
                      w_ff2[layer].astype(BF16), row(final_norm_g), final_norm=last)
    return h.reshape(b, s, d)
```

```python
import functools
import math

import jax
import jax.numpy as jnp
from jax import lax
from jax.experimental import pallas as pl
from jax.experimental.pallas import tpu as pltpu

F32 = jnp.float32
BF16 = jnp.bfloat16

RMS_EPS = 1e-6
LN_EPS = 1e-5

SB_HEADS = 8
SB_HEAD_DIM = 64
SB_WIDTH = SB_HEADS * SB_HEAD_DIM
HG_HEADS = 4
HG_HEAD_DIM = 128
HG_WIDTH = HG_HEADS * HG_HEAD_DIM
CONV_WIDTH = 31

LANES = 128
SUBLANES = 8
LOG2E = math.log2(math.e)
SB_BLOCK = 128
SB_UNROLL = 3
SB_UNDERFLOW_LOG2 = -92.0 * LOG2E
HG_STEP = 32
HG_SUB = 256
CONV_HIST = 32
VMEM_LIMIT = 56 * 1024 * 1024

_NT = (((1,), (1,)), ((), ()))
_TN = (((0,), (0,)), ((), ()))


def _rms(x, g):
    return x * lax.rsqrt(jnp.mean(x * x, axis=-1, keepdims=True) + RMS_EPS) * g


def _dot(a, b):
    return jnp.dot(a, b, preferred_element_type=F32)


def _const_spec(shape):
    nd = len(shape)
    return pl.BlockSpec(shape, lambda *_: (0,) * nd, pipeline_mode=pl.Buffered(1))


def _inproj_kernel(x_ref, g_ref, w_ref, o_ref, *, tn):
    u = _rms(x_ref[...], g_ref[...]).astype(BF16)
    for n in range(w_ref.shape[1] // tn):
        cols = slice(n * tn, (n + 1) * tn)
        o_ref[:, cols] = _dot(u, w_ref[:, cols]).astype(BF16)


def _inproj(x2d, g, w, *, tm=512, tn=512):
    t, d = x2d.shape
    n = w.shape[1]
    tm = min(tm, t)
    return pl.pallas_call(
        functools.partial(_inproj_kernel, tn=tn),
        grid=(t // tm,),
        in_specs=[pl.BlockSpec((tm, d), lambda i: (i, 0)),
                  _const_spec((1, d)),
                  _const_spec((d, n))],
        out_specs=pl.BlockSpec((tm, n), lambda i: (i, 0)),
        out_shape=jax.ShapeDtypeStruct((t, n), BF16),
        compiler_params=pltpu.CompilerParams(dimension_semantics=("arbitrary",),
                                             vmem_limit_bytes=VMEM_LIMIT),
        name="inproj",
    )(x2d, g, w)


def _sb_kernel(q_ref, k_ref, v_ref, o_ref, r_ref, acc_ref):
    qi = pl.program_id(1)
    blk = SB_BLOCK
    pairs = q_ref.shape[2] // LANES
    row = lax.broadcasted_iota(jnp.int32, (blk, blk), 0)
    col = lax.broadcasted_iota(jnp.int32, (blk, blk), 1)
    lo = col < SB_HEAD_DIM
    causal = col < row
    trow = lax.broadcasted_iota(jnp.int32, (blk, 2 * blk), 0)
    tcol = lax.broadcasted_iota(jnp.int32, (blk, 2 * blk), 1)
    tri_ones = jnp.where((trow > tcol) | (tcol >= blk), 1.0, 0.0).astype(BF16)
    zero = jnp.zeros((blk, LANES), BF16)

    def split(x):
        return (jnp.where(lo, x, zero), jnp.where(lo, zero, x))

    def rows_of(kb):
        return pl.ds(pl.multiple_of(kb * blk, blk), blk)

    def sweep(kbs, diag_first, fresh):
        nb = len(kbs)
        rows = [rows_of(kb) for kb in kbs]
        z_of, mid_of, vs_of = {}, {}, {}

        def stage_scores(p):
            lanes = slice(p * LANES, (p + 1) * LANES)
            q_h = split(q_ref[0, :, lanes])
            ks = [k_ref[0, rows[j], lanes] for j in range(nb)]
            vs_of[p] = [split(v_ref[0, rows[j], lanes]) for j in range(nb)]
            z_of[p] = [[lax.dot_general(q_h[hh], ks[j], _NT, preferred_element_type=F32)
                        for j in range(nb)] for hh in range(2)]

        def stage_sums(p):
            mid = []
            for hh in range(2):
                per_block = []
                for j in range(nb):
                    z = z_of[p][hh][j]
                    log_beta = jnp.minimum(z, 0.0) - jnp.log2(1.0 + jnp.exp2(-jnp.abs(z)))
                    log_keep = log_beta - z
                    if diag_first and j == 0:
                        log_keep = jnp.where(causal, log_keep, 0.0)
                    sums = _dot(log_keep.astype(BF16), tri_ones)
                    per_block.append((log_beta, sums[:, :blk], sums[:, blk:]))
                mid.append(per_block)
            mid_of[p] = mid
            del z_of[p]

        def stage_values(p):
            lanes = slice(p * LANES, (p + 1) * LANES)
            pv = None
            for hh in range(2):
                r = None if fresh else r_ref[2 * p + hh]
                for j in range(nb):
                    log_beta, later, total = mid_of[p][hh][j]
                    e = log_beta + later
                    if r is not None:
                        e = e + r
                    w = jnp.exp2(e)
                    if diag_first and j == 0:
                        w = jnp.where(causal, w, 0.0)
                    r = total if r is None else r + total
                    part = _dot(w.astype(BF16), vs_of[p][j][hh])
                    pv = part if pv is None else pv + part
                r_ref[2 * p + hh] = r
            if fresh:
                acc_ref[:, lanes] = pv
            else:
                acc_ref[:, lanes] += pv
            del mid_of[p], vs_of[p]

        for t in range(pairs + 2):
            if t < pairs:
                stage_scores(t)
            if 0 <= t - 1 < pairs:
                stage_sums(t - 1)
            if 0 <= t - 2 < pairs:
                stage_values(t - 2)

    @pl.when(qi < SB_UNROLL - 1)
    def _():
        sweep([qi], True, True)

    @pl.when(qi >= SB_UNROLL - 1)
    def _():
        sweep([qi - j for j in range(SB_UNROLL)], True, True)

    def cond(c):
        kb, m = c
        return jnp.logical_and(kb >= 0, m > SB_UNDERFLOW_LOG2)

    def body(c):
        kb, _ = c
        sweep([kb], False, False)
        return kb - 1, jnp.max(r_ref[...])

    start = jnp.where(qi >= SB_UNROLL - 1, qi - SB_UNROLL, qi - 1)
    lax.while_loop(cond, body, (start, jnp.max(r_ref[...])))
    o_ref[0] = acc_ref[...].astype(BF16)


def _sb_attention(proj3):
    b, s, _ = proj3.shape
    nq = s // SB_BLOCK
    kv_spec = lambda c: pl.BlockSpec((1, s, SB_WIDTH), lambda bi, qi: (bi, 0, c),
                                     pipeline_mode=pl.Buffered(1))
    return pl.pallas_call(
        _sb_kernel,
        grid=(b, nq),
        in_specs=[pl.BlockSpec((1, SB_BLOCK, SB_WIDTH), lambda bi, qi: (bi, qi, 0)),
                  kv_spec(1), kv_spec(2)],
        out_specs=pl.BlockSpec((1, SB_BLOCK, SB_WIDTH), lambda bi, qi: (bi, qi, 0)),
        out_shape=jax.ShapeDtypeStruct((b, s, SB_WIDTH), BF16),
        scratch_shapes=[pltpu.VMEM((SB_HEADS, SB_BLOCK, SB_BLOCK), F32),
                        pltpu.VMEM((SB_BLOCK, SB_WIDTH), F32)],
        compiler_params=pltpu.CompilerParams(dimension_semantics=("arbitrary", "arbitrary"),
                                             vmem_limit_bytes=VMEM_LIMIT),
        name="sb_attention",
    )(proj3, proj3, proj3)


def _hg_kernel(q_ref, f_ref, i_ref, gate_ref, lbl_ref, ng_ref, o_ref, state_ref, *, layer):
    si = pl.program_id(2)
    step, sub = HG_STEP, HG_SUB
    half = step // 2

    @pl.when(si == 0)
    def _():
        state_ref[...] = jnp.zeros_like(state_ref)

    logits = lbl_ref[...]
    e = jnp.exp(logits - jnp.max(logits, axis=0, keepdims=True))
    lb = jnp.sum(e[:layer + 1], axis=0, keepdims=True) / jnp.sum(e, axis=0, keepdims=True)
    ng = ng_ref[0]

    r = lax.broadcasted_iota(jnp.int32, (sub, sub), 0)
    c = lax.broadcasted_iota(jnp.int32, (sub, sub), 1)
    same = (r // step) == (c // step)
    incl = jnp.where(same & (c <= r), 1.0, 0.0)
    to_mid = jnp.where(same & (c % step < half), 1.0, 0.0)
    m_q = (incl - to_mid).astype(BF16)
    m_k = jnp.where(same & (c > r), 1.0, 0.0).astype(BF16)
    m_g = incl.astype(BF16)
    tr = lax.broadcasted_iota(jnp.int32, (step, step), 0)
    tc = lax.broadcasted_iota(jnp.int32, (step, step), 1)
    tril = tr >= tc
    per_grp = LANES // step
    lane_step = lax.broadcasted_iota(jnp.int32, (HG_HEAD_DIM, LANES), 1) // step

    def body(j, carry):
        rows = pl.ds(pl.multiple_of(j * sub, sub), sub)
        q = q_ref[0, rows, :].astype(F32)
        f = lb + (1.0 - lb) * jax.nn.sigmoid(f_ref[0, rows, :].astype(F32))
        v = i_ref[0, rows, :]
        g = jnp.log(f)
        kk = 1.0 - f
        g_hi = g.astype(BF16)
        g_lo = (g - g_hi.astype(F32)).astype(BF16)
        d_q = _dot(m_q, g_hi) + _dot(m_q, g_lo)
        d_k = _dot(m_k, g_hi) + _dot(m_k, g_lo)
        e_g = jnp.exp(_dot(m_g, g_hi) + _dot(m_g, g_lo))
        q_intra = (q * jnp.exp(d_q)).astype(BF16)
        k_intra = (kk * jnp.exp(-d_q)).astype(BF16)
        k_state = (kk * jnp.exp(d_k)).astype(BF16)
        q_inter = (q * e_g).astype(BF16)
        steps = [slice(n * step, (n + 1) * step) for n in range(sub // step)]
        scores = [lax.dot_general(q_intra[sl], k_intra[sl], _NT, preferred_element_type=F32)
                  for sl in steps]
        upd_t = [lax.dot_general(v[sl], k_state[sl], _TN, preferred_element_type=F32)
                 for sl in steps]
        intra = [_dot(jnp.where(tril, sc, 0.0).astype(BF16), v[sl])
                 for sc, sl in zip(scores, steps)]
        state_t = state_ref[...]
        inter = []
        for gi in range(sub // LANES):
            q_grp = q_inter[gi * LANES:(gi + 1) * LANES]
            o_t = None
            for m in range(per_grp):
                n = gi * per_grp + m
                res = lax.dot_general(state_t.astype(BF16), q_grp, _NT, preferred_element_type=F32)
                o_t = res if o_t is None else jnp.where(lane_step == m, res, o_t)
                state_t = state_t * e_g[(n + 1) * step - 1:(n + 1) * step] + upd_t[n]
            inter.append(o_t.T)
        state_ref[...] = state_t
        o = jnp.concatenate(intra, axis=0) + jnp.concatenate(inter, axis=0)
        o = o * lax.rsqrt(jnp.mean(o * o, axis=-1, keepdims=True) + RMS_EPS) * ng
        o = o * jax.nn.silu(gate_ref[0, rows, :].astype(F32))
        o_ref[0, rows, :] = o.astype(BF16)
        return carry

    lax.fori_loop(0, q_ref.shape[1] // sub, body, 0, unroll=True)


def _hgrn2(proj3, lb_logits, norm_g, *, layer, ts=1024):
    b, s, _ = proj3.shape
    ts = min(ts, s)
    base = 3 * SB_WIDTH // LANES
    depth1 = lb_logits.shape[0]

    def col(k):
        return pl.BlockSpec((1, ts, LANES), lambda bi, h, si: (bi, si, base + k * HG_HEADS + h))

    return pl.pallas_call(
        functools.partial(_hg_kernel, layer=layer),
        grid=(b, HG_HEADS, s // ts),
        in_specs=[col(0), col(1), col(2), col(3),
                  pl.BlockSpec((depth1, LANES), lambda bi, h, si: (0, h)),
                  pl.BlockSpec((1, 1, LANES), lambda bi, h, si: (h, 0, 0))],
        out_specs=pl.BlockSpec((1, ts, LANES), lambda bi, h, si: (bi, si, h)),
        out_shape=jax.ShapeDtypeStruct((b, s, HG_WIDTH), BF16),
        scratch_shapes=[pltpu.VMEM((HG_HEAD_DIM, HG_HEAD_DIM), F32)],
        compiler_params=pltpu.CompilerParams(
            dimension_semantics=("arbitrary", "arbitrary", "arbitrary"),
            vmem_limit_bytes=VMEM_LIMIT),
        name="hgrn2",
    )(proj3, proj3, proj3, proj3, lb_logits, norm_g.reshape(HG_HEADS, 1, HG_HEAD_DIM))


def _proj_mlp_kernel(ma_ref, mb_ref, h_ref, wp_ref, bp_ref, g_ref, w1_ref, w2_ref, gf_ref,
                     o_ref, *, tf, final_norm):
    half = ma_ref.shape[1]
    h1 = (h_ref[...] + _dot(ma_ref[...], wp_ref[:half, :]) + _dot(mb_ref[...], wp_ref[half:, :])
          + bp_ref[...])
    u = _rms(h1, g_ref[...]).astype(BF16)
    acc = h1
    for c in range(w1_ref.shape[1] // tf):
        a = jnp.maximum(_dot(u, w1_ref[:, c * tf:(c + 1) * tf]), 0.0)
        acc = acc + _dot((a * a).astype(BF16), w2_ref[c * tf:(c + 1) * tf, :])
    if final_norm:
        acc = _rms(acc, gf_ref[...])
    o_ref[...] = acc


def _proj_mlp(ma, mb, cols_a, cols_b, h2d, wp, bp, g, w1, w2, gf, *, final_norm, tm=512, tf=1024):
    t, d = h2d.shape
    half = d // 2
    dff = w1.shape[1]
    tm = min(tm, t)
    return pl.pallas_call(
        functools.partial(_proj_mlp_kernel, tf=tf, final_norm=final_norm),
        grid=(t // tm,),
        in_specs=[pl.BlockSpec((tm, half), lambda i: (i, cols_a)),
                  pl.BlockSpec((tm, half), lambda i: (i, cols_b)),
                  pl.BlockSpec((tm, d), lambda i: (i, 0)),
                  _const_spec((d, d)), _const_spec((1, d)), _const_spec((1, d)),
                  _const_spec((d, dff)), _const_spec((dff, d)), _const_spec((1, d))],
        out_specs=pl.BlockSpec((tm, d), lambda i: (i, 0)),
        out_shape=jax.ShapeDtypeStruct((t, d), F32),
        compiler_params=pltpu.CompilerParams(dimension_semantics=("arbitrary",),
                                             vmem_limit_bytes=VMEM_LIMIT),
        name="proj_mlp_final" if final_norm else "proj_mlp",
    )(ma, mb, h2d, wp, bp, g, w1, w2, gf)


def _conv_kernel(h_ref, g_ref, wg_ref, bg_ref, wdw_ref, bdw_ref, lng_ref, lnb_ref, o_ref,
                 sh_ref, y_ref, *, rc):
    si = pl.program_id(1)
    ts, d = h_ref.shape[1], h_ref.shape[2]
    hist = CONV_HIST
    n_rows = ts + hist

    @pl.when(si == 0)
    def _():
        sh_ref[0, 0:hist, :] = jnp.zeros((hist, d), F32)

    @pl.when(si > 0)
    def _():
        sh_ref[0, 0:hist, :] = sh_ref[0, ts:n_rows, :]

    u = _rms(h_ref[0], g_ref[...]).astype(BF16)
    lin_a = _dot(u, wg_ref[:, :d]) + bg_ref[:, :d]
    lin_b = _dot(u, wg_ref[:, d:]) + bg_ref[:, d:]
    sh_ref[0, hist:, :] = lin_a * jax.nn.sigmoid(lin_b)

    for l0 in range(0, d, LANES):
        lanes = slice(l0, l0 + LANES)
        col = sh_ref[0, :, lanes]
        for sft in range(1, SUBLANES):
            sh_ref[sft, :, lanes] = pltpu.roll(col, sft, axis=0)
        for r0 in range(0, ts, rc):
            acc = jnp.broadcast_to(bdw_ref[:, lanes], (rc, LANES))
            for dly in range(CONV_WIDTH):
                tap = CONV_WIDTH - 1 - dly
                start = hist + r0 - SUBLANES * (dly // SUBLANES)
                acc = acc + wdw_ref[tap:tap + 1, lanes] * sh_ref[dly % SUBLANES, start:start + rc, lanes]
            y_ref[r0:r0 + rc, lanes] = acc

    y = y_ref[...]
    mu = jnp.mean(y, axis=-1, keepdims=True)
    yc = y - mu
    var = jnp.mean(yc * yc, axis=-1, keepdims=True)
    yn = yc * lax.rsqrt(var + LN_EPS) * lng_ref[...] + lnb_ref[...]
    o_ref[0] = (yn * jax.nn.sigmoid(yn)).astype(BF16)


def _conv_front(h3, g, wg, bg, wdw, bdw, lng, lnb, *, ts=256, rc=64):
    b, s, d = h3.shape
    ts = min(ts, s)
    return pl.pallas_call(
        functools.partial(_conv_kernel, rc=rc),
        grid=(b, s // ts),
        in_specs=[pl.BlockSpec((1, ts, d), lambda bi, si: (bi, si, 0)),
                  _const_spec((1, d)), _const_spec((d, 2 * d)), _const_spec((1, 2 * d)),
                  _const_spec((CONV_WIDTH, d)), _const_spec((1, d)), _const_spec((1, d)),
                  _const_spec((1, d))],
        out_specs=pl.BlockSpec((1, ts, d), lambda bi, si: (bi, si, 0)),
        out_shape=jax.ShapeDtypeStruct((b, s, d), BF16),
        scratch_shapes=[pltpu.VMEM((SUBLANES, ts + CONV_HIST, d), F32), pltpu.VMEM((ts, d), F32)],
        compiler_params=pltpu.CompilerParams(dimension_semantics=("arbitrary", "arbitrary"),
                                             vmem_limit_bytes=VMEM_LIMIT),
        name="conv_front",
    )(h3, g, wg, bg, wdw, bdw, lng, lnb)


def kernel(x, norm_mix_g, norm_ffn_g, w_in_ab, w_out_ab, hg_lb_logits, hg_norm_g, conv_w_glu,
           conv_b_glu, conv_w_dw, conv_b_dw, conv_ln_g, conv_ln_b, conv_w_pw, conv_b_pw,
           w_ff1, w_ff2, final_norm_g):
    b, s, d = x.shape
    depth = norm_mix_g.shape[0]
    row = lambda a: a.reshape(1, -1).astype(F32)
    zeros_d = jnp.zeros((1, d), F32)
    h = x.reshape(b * s, d)
    for layer in range(depth):
        j = layer // 2
        last = layer == depth - 1
        if layer % 2 == 0:
            q_scale = SB_HEAD_DIM ** -0.5 * LOG2E
            scale = jnp.where(jnp.arange(w_in_ab.shape[2]) < SB_WIDTH, q_scale, 1.0)
            w_in = (w_in_ab[j] * scale).astype(BF16)
            proj = _inproj(h, row(norm_mix_g[layer]), w_in).reshape(b, s, -1)
            o_sb = _sb_attention(proj).reshape(b * s, SB_WIDTH)
            o_hg = _hgrn2(proj, hg_lb_logits.astype(F32), hg_norm_g[j].astype(F32),
                          layer=layer).reshape(b * s, HG_WIDTH)
            mix = (o_sb, o_hg, 0, 0)
            wp, bp = w_out_ab[j].astype(BF16), zeros_d
        else:
            y = _conv_front(h.reshape(b, s, d), row(norm_mix_g[layer]),
                            conv_w_glu[j].astype(BF16), row(conv_b_glu[j]),
                            conv_w_dw[j].astype(F32), row(conv_b_dw[j]), row(conv_ln_g[j]),
                            row(conv_ln_b[j])).reshape(b * s, d)
            mix = (y, y, 0, 1)
            wp, bp = conv_w_pw[j].astype(BF16), row(conv_b_pw[j])
        h = _proj_mlp(*mix, h, wp, bp, row(norm_ffn_g[layer]), w_ff1[layer].astype(BF16),
                      w_ff2[layer].astype(BF16), row(final_norm_g), final_norm=last)
    return h.reshape(b, s, d)
```

```python
import functools
import math

import jax
import jax.numpy as jnp
from jax import lax
from jax.experimental import pallas as pl
from jax.experimental.pallas import tpu as pltpu

F32 = jnp.float32
BF16 = jnp.bfloat16

RMS_EPS = 1e-6
LN_EPS = 1e-5

SB_HEADS = 8
SB_HEAD_DIM = 64
SB_WIDTH = SB_HEADS * SB_HEAD_DIM
HG_HEADS = 4
HG_HEAD_DIM = 128
HG_WIDTH = HG_HEADS * HG_HEAD_DIM
CONV_WIDTH = 31

LANES = 128
SUBLANES = 8
LOG2E = math.log2(math.e)
SB_BLOCK = 128
SB_HALF = SB_BLOCK // 2
SB_UNDERFLOW_LOG2 = -92.0 * LOG2E
HG_STEP = 32
HG_SUB = 256
CONV_HIST = 32
VMEM_LIMIT = 56 * 1024 * 1024

_NT = (((1,), (1,)), ((), ()))
_TN = (((0,), (0,)), ((), ()))


def _rms(x, g):
    return x * lax.rsqrt(jnp.mean(x * x, axis=-1, keepdims=True) + RMS_EPS) * g


def _dot(a, b):
    return jnp.dot(a, b, preferred_element_type=F32)


def _const_spec(shape):
    nd = len(shape)
    return pl.BlockSpec(shape, lambda *_: (0,) * nd, pipeline_mode=pl.Buffered(1))


def _inproj_kernel(x_ref, g_ref, w_ref, o_ref, *, tn):
    u = _rms(x_ref[...], g_ref[...]).astype(BF16)
    for n in range(w_ref.shape[1] // tn):
        cols = slice(n * tn, (n + 1) * tn)
        o_ref[:, cols] = _dot(u, w_ref[:, cols]).astype(BF16)


def _inproj(x2d, g, w, *, tm=512, tn=512):
    t, d = x2d.shape
    n = w.shape[1]
    tm = min(tm, t)
    return pl.pallas_call(
        functools.partial(_inproj_kernel, tn=tn),
        grid=(t // tm,),
        in_specs=[pl.BlockSpec((tm, d), lambda i: (i, 0)),
                  _const_spec((1, d)),
                  _const_spec((d, n))],
        out_specs=pl.BlockSpec((tm, n), lambda i: (i, 0)),
        out_shape=jax.ShapeDtypeStruct((t, n), BF16),
        compiler_params=pltpu.CompilerParams(dimension_semantics=("arbitrary",),
                                             vmem_limit_bytes=VMEM_LIMIT),
        name="inproj",
    )(x2d, g, w)


def _sb_kernel(q_ref, k_ref, v_ref, o_ref, r_ref, acc_ref):
    blk = SB_BLOCK
    n_sub = q_ref.shape[1] // blk
    qi0 = pl.program_id(1) * n_sub
    pairs = q_ref.shape[2] // LANES
    row = lax.broadcasted_iota(jnp.int32, (blk, blk), 0)
    col = lax.broadcasted_iota(jnp.int32, (blk, blk), 1)
    causal = col < row
    trow = lax.broadcasted_iota(jnp.int32, (blk, 2 * blk), 0)
    tcol = lax.broadcasted_iota(jnp.int32, (blk, 2 * blk), 1)
    tri_ones = jnp.where((trow > tcol) | (tcol >= blk), 1.0, 0.0).astype(BF16)

    def split(x):
        lo = lax.broadcasted_iota(jnp.int32, x.shape, 1) < SB_HEAD_DIM
        zero = jnp.zeros_like(x)
        return (jnp.where(lo, x, zero), jnp.where(lo, zero, x))

    def sweep(groups, fresh):
        units = [(p, groups) for p in range(pairs)]
        z_of, mid_of, vs_of = {}, {}, {}

        def stage_scores(u):
            p, grps = units[u]
            lanes = slice(p * LANES, (p + 1) * LANES)
            vs_of[u], z_of[u] = [], []
            for q_rows, windows, _ in grps:
                q2 = jnp.concatenate(split(q_ref[0, q_rows, lanes]), axis=0)
                vs_of[u].append([v_ref[0, w, lanes] for w in windows])
                z_of[u].append([lax.dot_general(q2, k_ref[0, w, lanes], _NT,
                                                preferred_element_type=F32) for w in windows])

        def stage_sums(u):
            _, grps = units[u]
            log_betas, log_keeps = [], []
            for gi, (_, windows, mask) in enumerate(grps):
                mask2 = None if mask is None else jnp.concatenate([mask, mask], axis=0)
                for j in range(len(windows)):
                    z = z_of[u][gi][j]
                    log_beta = jnp.minimum(z, 0.0) - jnp.log2(1.0 + jnp.exp2(-jnp.abs(z)))
                    log_keep = log_beta - z
                    if mask2 is not None and j == 0:
                        log_keep = jnp.where(mask2, log_keep, 0.0)
                    log_betas.append(log_beta)
                    log_keeps.append(log_keep.astype(BF16))
            sums = _dot(jnp.concatenate(log_keeps, axis=0), tri_ones)
            mid_of[u], at = [], 0
            for log_beta in log_betas:
                n = log_beta.shape[0]
                mid_of[u].append((log_beta, sums[at:at + n, :blk], sums[at:at + n, blk:]))
                at += n
            del z_of[u]

        def stage_values(u):
            p, grps = units[u]
            lanes = slice(p * LANES, (p + 1) * LANES)
            mids = iter(mid_of[u])
            for gi, (q_rows, windows, mask) in enumerate(grps):
                n = q_rows.stop - q_rows.start
                mask2 = None if mask is None else jnp.concatenate([mask, mask], axis=0)
                r = None if fresh else jnp.concatenate(
                    [r_ref[2 * p, q_rows, :], r_ref[2 * p + 1, q_rows, :]], axis=0)
                pv2 = None
                for j in range(len(windows)):
                    log_beta, later, total = next(mids)
                    e = log_beta + later
                    if r is not None:
                        e = e + r
                    w = jnp.exp2(e)
                    if mask2 is not None and j == 0:
                        w = jnp.where(mask2, w, 0.0)
                    r = total if r is None else r + total
                    part = _dot(w.astype(BF16), vs_of[u][gi][j])
                    pv2 = part if pv2 is None else pv2 + part
                r_ref[2 * p, q_rows, :] = r[:n]
                r_ref[2 * p + 1, q_rows, :] = r[n:]
                lo = lax.broadcasted_iota(jnp.int32, (n, LANES), 1) < SB_HEAD_DIM
                pv = jnp.where(lo, pv2[:n], pv2[n:])
                if fresh:
                    acc_ref[q_rows, lanes] = pv
                else:
                    acc_ref[q_rows, lanes] += pv
            del mid_of[u], vs_of[u]

        for t in range(len(units) + 2):
            if t < len(units):
                stage_scores(t)
            if 0 <= t - 1 < len(units):
                stage_sums(t - 1)
            if 0 <= t - 2 < len(units):
                stage_values(t - 2)

    def key_block(kb):
        return pl.ds(pl.multiple_of(kb * blk, blk), blk)

    def window_sweep():
        win = 2 * blk
        wr = lax.broadcasted_iota(jnp.int32, (win, win), 0)
        wc = lax.broadcasted_iota(jnp.int32, (win, win), 1)
        tri = jnp.where(wr > wc, 1.0, 0.0).astype(BF16)
        hr = lax.broadcasted_iota(jnp.int32, (2 * SB_HALF, blk), 0) % SB_HALF
        hc = lax.broadcasted_iota(jnp.int32, (2 * SB_HALF, blk), 1)
        newest_ok = hc < hr + SB_HALF
        halves = [(slice(first, first + SB_HALF),
                   pl.ds(pl.multiple_of(qi0 * blk + first + SB_HALF - win, SB_HALF), win))
                  for first in range(0, n_sub * blk, SB_HALF)]
        z_of, mid_of, v_of = {}, {}, {}
        worst = [jnp.full((1, 1), -jnp.inf, F32)]

        def masked(x):
            return jnp.concatenate([x[:, :blk], jnp.where(newest_ok, x[:, blk:], 0.0)], axis=1)

        def stage_scores(p):
            lanes = slice(p * LANES, (p + 1) * LANES)
            z_of[p], v_of[p] = [], []
            for q_rows, keys in halves:
                q2 = jnp.concatenate(split(q_ref[0, q_rows, lanes]), axis=0)
                v_of[p].append(v_ref[0, keys, lanes])
                z_of[p].append(lax.dot_general(q2, k_ref[0, keys, lanes], _NT,
                                               preferred_element_type=F32))

        def stage_sums(p):
            log_betas, log_keeps = [], []
            for z in z_of[p]:
                log_beta = jnp.minimum(z, 0.0) - jnp.log2(1.0 + jnp.exp2(-jnp.abs(z)))
                log_keep = masked(log_beta - z)
                log_betas.append(log_beta)
                log_keeps.append(log_keep)
            stacked = jnp.concatenate(log_keeps, axis=0)
            later = _dot(stacked.astype(BF16), tri)
            totals = later[:, 0:1] + stacked[:, 0:1]
            worst[0] = jnp.maximum(worst[0], jnp.max(totals, axis=0, keepdims=True))
            n = 2 * SB_HALF
            mid_of[p] = [(lb, later[i * n:(i + 1) * n]) for i, lb in enumerate(log_betas)]
            del z_of[p]

        def stage_values(p):
            lanes = slice(p * LANES, (p + 1) * LANES)
            lo = lax.broadcasted_iota(jnp.int32, (SB_HALF, LANES), 1) < SB_HEAD_DIM
            for (q_rows, _), (log_beta, later), v in zip(halves, mid_of[p], v_of[p]):
                w = masked(jnp.exp2(log_beta + later))
                pv2 = _dot(w.astype(BF16), v)
                acc_ref[q_rows, lanes] = jnp.where(lo, pv2[:SB_HALF], pv2[SB_HALF:])
            del mid_of[p], v_of[p]

        for t in range(pairs + 2):
            if t < pairs:
                stage_scores(t)
            if 0 <= t - 1 < pairs:
                stage_sums(t - 1)
            if 0 <= t - 2 < pairs:
                stage_values(t - 2)
        r_ref[0, 0:SUBLANES, :] = jnp.broadcast_to(worst[0], (SUBLANES, blk))

    @pl.when(qi0 < 2)
    def _():
        r_ref[0, 0:SUBLANES, :] = jnp.zeros((SUBLANES, blk), F32)

    @pl.when(qi0 >= 2)
    def _():
        window_sweep()

    redo = jnp.max(r_ref[0, 0:SUBLANES, :]) > SB_UNDERFLOW_LOG2

    @pl.when(redo)
    def _():
        for sub in range(n_sub):
            q_rows = slice(sub * blk, (sub + 1) * blk)
            sweep([(q_rows, [key_block(qi0 + sub)], causal)], True)

            def cond(c):
                kb, m = c
                return jnp.logical_and(kb >= 0, m > SB_UNDERFLOW_LOG2)

            def body(c, q_rows=q_rows):
                kb, _ = c
                sweep([(q_rows, [key_block(kb)], None)], False)
                return kb - 1, jnp.max(r_ref[:, q_rows, :])

            lax.while_loop(cond, body, (qi0 + sub - 1, jnp.max(r_ref[:, q_rows, :])))

    o_ref[0] = acc_ref[...].astype(BF16)


def _sb_attention(proj3, *, q_rows=2 * SB_BLOCK):
    b, s, _ = proj3.shape
    q_rows = min(q_rows, s)
    kv_spec = lambda c: pl.BlockSpec((1, s, SB_WIDTH), lambda bi, qi: (bi, 0, c),
                                     pipeline_mode=pl.Buffered(1))
    return pl.pallas_call(
        _sb_kernel,
        grid=(b, s // q_rows),
        in_specs=[pl.BlockSpec((1, q_rows, SB_WIDTH), lambda bi, qi: (bi, qi, 0)),
                  kv_spec(1), kv_spec(2)],
        out_specs=pl.BlockSpec((1, q_rows, SB_WIDTH), lambda bi, qi: (bi, qi, 0)),
        out_shape=jax.ShapeDtypeStruct((b, s, SB_WIDTH), BF16),
        scratch_shapes=[pltpu.VMEM((SB_HEADS, q_rows, SB_BLOCK), F32),
                        pltpu.VMEM((q_rows, SB_WIDTH), F32)],
        compiler_params=pltpu.CompilerParams(dimension_semantics=("arbitrary", "arbitrary"),
                                             vmem_limit_bytes=VMEM_LIMIT),
        name="sb_attention",
    )(proj3, proj3, proj3)


def _hg_kernel(q_ref, f_ref, i_ref, gate_ref, lbl_ref, ng_ref, o_ref, state_ref, *, layer):
    si = pl.program_id(2)
    step, sub = HG_STEP, HG_SUB
    half = step // 2

    @pl.when(si == 0)
    def _():
        state_ref[...] = jnp.zeros_like(state_ref)

    logits = lbl_ref[...]
    e = jnp.exp(logits - jnp.max(logits, axis=0, keepdims=True))
    lb = jnp.sum(e[:layer + 1], axis=0, keepdims=True) / jnp.sum(e, axis=0, keepdims=True)
    ng = ng_ref[0]

    r = lax.broadcasted_iota(jnp.int32, (sub, sub), 0)
    c = lax.broadcasted_iota(jnp.int32, (sub, sub), 1)
    same = (r // step) == (c // step)
    incl = jnp.where(same & (c <= r), 1.0, 0.0)
    to_mid = jnp.where(same & (c % step < half), 1.0, 0.0)
    m_q = (incl - to_mid).astype(BF16)
    m_k = jnp.where(same & (c > r), 1.0, 0.0).astype(BF16)
    m_g = incl.astype(BF16)
    tr = lax.broadcasted_iota(jnp.int32, (step, step), 0)
    tc = lax.broadcasted_iota(jnp.int32, (step, step), 1)
    tril = tr >= tc
    per_grp = LANES // step
    lane_step = lax.broadcasted_iota(jnp.int32, (HG_HEAD_DIM, LANES), 1) // step

    def body(j, carry):
        rows = pl.ds(pl.multiple_of(j * sub, sub), sub)
        q = q_ref[0, rows, :].astype(F32)
        f = lb + (1.0 - lb) * jax.nn.sigmoid(f_ref[0, rows, :].astype(F32))
        v = i_ref[0, rows, :]
        g = jnp.log(f)
        kk = 1.0 - f
        g_hi = g.astype(BF16)
        g_lo = (g - g_hi.astype(F32)).astype(BF16)
        d_q = _dot(m_q, g_hi) + _dot(m_q, g_lo)
        d_k = _dot(m_k, g_hi) + _dot(m_k, g_lo)
        e_g = jnp.exp(_dot(m_g, g_hi) + _dot(m_g, g_lo))
        q_intra = (q * jnp.exp(d_q)).astype(BF16)
        k_intra = (kk * jnp.exp(-d_q)).astype(BF16)
        k_state = (kk * jnp.exp(d_k)).astype(BF16)
        q_inter = (q * e_g).astype(BF16)
        steps = [slice(n * step, (n + 1) * step) for n in range(sub // step)]
        scores = [lax.dot_general(q_intra[sl], k_intra[sl], _NT, preferred_element_type=F32)
                  for sl in steps]
        upd_t = [lax.dot_general(v[sl], k_state[sl], _TN, preferred_element_type=F32)
                 for sl in steps]
        intra = [_dot(jnp.where(tril, sc, 0.0).astype(BF16), v[sl])
                 for sc, sl in zip(scores, steps)]
        state_t = state_ref[...]
        inter = []
        for gi in range(sub // LANES):
            q_grp = q_inter[gi * LANES:(gi + 1) * LANES]
            o_t = None
            for m in range(per_grp):
                n = gi * per_grp + m
                res = lax.dot_general(state_t.astype(BF16), q_grp, _NT, preferred_element_type=F32)
                o_t = res if o_t is None else jnp.where(lane_step == m, res, o_t)
                state_t = state_t * e_g[(n + 1) * step - 1:(n + 1) * step] + upd_t[n]
            inter.append(o_t.T)
        state_ref[...] = state_t
        o = jnp.concatenate(intra, axis=0) + jnp.concatenate(inter, axis=0)
        o = o * lax.rsqrt(jnp.mean(o * o, axis=-1, keepdims=True) + RMS_EPS) * ng
        o = o * jax.nn.silu(gate_ref[0, rows, :].astype(F32))
        o_ref[0, rows, :] = o.astype(BF16)
        return carry

    lax.fori_loop(0, q_ref.shape[1] // sub, body, 0, unroll=True)


def _hgrn2(proj3, lb_logits, norm_g, *, layer, ts=2048):
    b, s, _ = proj3.shape
    ts = min(ts, s)
    base = 3 * SB_WIDTH // LANES
    depth1 = lb_logits.shape[0]

    def col(k):
        return pl.BlockSpec((1, ts, LANES), lambda bi, h, si: (bi, si, base + k * HG_HEADS + h))

    return pl.pallas_call(
        functools.partial(_hg_kernel, layer=layer),
        grid=(b, HG_HEADS, s // ts),
        in_specs=[col(0), col(1), col(2), col(3),
                  pl.BlockSpec((depth1, LANES), lambda bi, h, si: (0, h)),
                  pl.BlockSpec((1, 1, LANES), lambda bi, h, si: (h, 0, 0))],
        out_specs=pl.BlockSpec((1, ts, LANES), lambda bi, h, si: (bi, si, h)),
        out_shape=jax.ShapeDtypeStruct((b, s, HG_WIDTH), BF16),
        scratch_shapes=[pltpu.VMEM((HG_HEAD_DIM, HG_HEAD_DIM), F32)],
        compiler_params=pltpu.CompilerParams(
            dimension_semantics=("arbitrary", "arbitrary", "arbitrary"),
            vmem_limit_bytes=VMEM_LIMIT),
        name="hgrn2",
    )(proj3, proj3, proj3, proj3, lb_logits, norm_g.reshape(HG_HEADS, 1, HG_HEAD_DIM))


def _proj_mlp_kernel(ma_ref, mb_ref, h_ref, wp_ref, bp_ref, g_ref, w1_ref, w2_ref, gf_ref,
                     o_ref, *, tf, final_norm):
    half = ma_ref.shape[1]
    h1 = (h_ref[...] + _dot(ma_ref[...], wp_ref[:half, :]) + _dot(mb_ref[...], wp_ref[half:, :])
          + bp_ref[...])
    u = _rms(h1, g_ref[...]).astype(BF16)
    acc = h1
    for c in range(w1_ref.shape[1] // tf):
        a = jnp.maximum(_dot(u, w1_ref[:, c * tf:(c + 1) * tf]), 0.0)
        acc = acc + _dot((a * a).astype(BF16), w2_ref[c * tf:(c + 1) * tf, :])
    if final_norm:
        acc = _rms(acc, gf_ref[...])
    o_ref[...] = acc


def _proj_mlp(ma, mb, cols_a, cols_b, h2d, wp, bp, g, w1, w2, gf, *, final_norm, tm=512, tf=1024):
    t, d = h2d.shape
    half = d // 2
    dff = w1.shape[1]
    tm = min(tm, t)
    return pl.pallas_call(
        functools.partial(_proj_mlp_kernel, tf=tf, final_norm=final_norm),
        grid=(t // tm,),
        in_specs=[pl.BlockSpec((tm, half), lambda i: (i, cols_a)),
                  pl.BlockSpec((tm, half), lambda i: (i, cols_b)),
                  pl.BlockSpec((tm, d), lambda i: (i, 0)),
                  _const_spec((d, d)), _const_spec((1, d)), _const_spec((1, d)),
                  _const_spec((d, dff)), _const_spec((dff, d)), _const_spec((1, d))],
        out_specs=pl.BlockSpec((tm, d), lambda i: (i, 0)),
        out_shape=jax.ShapeDtypeStruct((t, d), F32),
        compiler_params=pltpu.CompilerParams(dimension_semantics=("arbitrary",),
                                             vmem_limit_bytes=VMEM_LIMIT),
        name="proj_mlp_final" if final_norm else "proj_mlp",
    )(ma, mb, h2d, wp, bp, g, w1, w2, gf)


def _conv_kernel(h_ref, g_ref, wg_ref, bg_ref, wdw_ref, bdw_ref, lng_ref, lnb_ref, o_ref,
                 sh_ref, y_ref, *, rc):
    si = pl.program_id(1)
    ts, d = h_ref.shape[1], h_ref.shape[2]
    hist = CONV_HIST
    n_rows = ts + hist

    @pl.when(si == 0)
    def _():
        sh_ref[0, 0:hist, :] = jnp.zeros((hist, d), F32)

    @pl.when(si > 0)
    def _():
        sh_ref[0, 0:hist, :] = sh_ref[0, ts:n_rows, :]

    u = _rms(h_ref[0], g_ref[...]).astype(BF16)
    lin_a = _dot(u, wg_ref[:, :d]) + bg_ref[:, :d]
    lin_b = _dot(u, wg_ref[:, d:]) + bg_ref[:, d:]
    sh_ref[0, hist:, :] = lin_a * jax.nn.sigmoid(lin_b)

    for l0 in range(0, d, LANES):
        lanes = slice(l0, l0 + LANES)
        col = sh_ref[0, :, lanes]
        for sft in range(1, SUBLANES):
            sh_ref[sft, :, lanes] = pltpu.roll(col, sft, axis=0)
        for r0 in range(0, ts, rc):
            acc = jnp.broadcast_to(bdw_ref[:, lanes], (rc, LANES))
            for dly in range(CONV_WIDTH):
                tap = CONV_WIDTH - 1 - dly
                start = hist + r0 - SUBLANES * (dly // SUBLANES)
                acc = acc + wdw_ref[tap:tap + 1, lanes] * sh_ref[dly % SUBLANES, start:start + rc, lanes]
            y_ref[r0:r0 + rc, lanes] = acc

    y = y_ref[...]
    mu = jnp.mean(y, axis=-1, keepdims=True)
    yc = y - mu
    var = jnp.mean(yc * yc, axis=-1, keepdims=True)
    yn = yc * lax.rsqrt(var + LN_EPS) * lng_ref[...] + lnb_ref[...]
    o_ref[0] = (yn * jax.nn.sigmoid(yn)).astype(BF16)


def _conv_front(h3, g, wg, bg, wdw, bdw, lng, lnb, *, ts=256, rc=64):
    b, s, d = h3.shape
    ts = min(ts, s)
    return pl.pallas_call(
        functools.partial(_conv_kernel, rc=rc),
        grid=(b, s // ts),
        in_specs=[pl.BlockSpec((1, ts, d), lambda bi, si: (bi, si, 0)),
                  _const_spec((1, d)), _const_spec((d, 2 * d)), _const_spec((1, 2 * d)),
                  _const_spec((CONV_WIDTH, d)), _const_spec((1, d)), _const_spec((1, d)),
                  _const_spec((1, d))],
        out_specs=pl.BlockSpec((1, ts, d), lambda bi, si: (bi, si, 0)),
        out_shape=jax.ShapeDtypeStruct((b, s, d), BF16),
        scratch_shapes=[pltpu.VMEM((SUBLANES, ts + CONV_HIST, d), F32), pltpu.VMEM((ts, d), F32)],
        compiler_params=pltpu.CompilerParams(dimension_semantics=("arbitrary", "arbitrary"),
                                             vmem_limit_bytes=VMEM_LIMIT),
        name="conv_front",
    )(h3, g, wg, bg, wdw, bdw, lng, lnb)


def kernel(x, norm_mix_g, norm_ffn_g, w_in_ab, w_out_ab, hg_lb_logits, hg_norm_g, conv_w_glu,
           conv_b_glu, conv_w_dw, conv_b_dw, conv_ln_g, conv_ln_b, conv_w_pw, conv_b_pw,
           w_ff1, w_ff2, final_norm_g):
    b, s, d = x.shape
    depth = norm_mix_g.shape[0]
    row = lambda a: a.reshape(1, -1).astype(F32)
    zeros_d = jnp.zeros((1, d), F32)
    h = x.reshape(b * s, d)
    for layer in range(depth):
        j = layer // 2
        last = layer == depth - 1
        if layer % 2 == 0:
            q_scale = SB_HEAD_DIM ** -0.5 * LOG2E
            scale = jnp.where(jnp.arange(w_in_ab.shape[2]) < SB_WIDTH, q_scale, 1.0)
            w_in = (w_in_ab[j] * scale).astype(BF16)
            proj = _inproj(h, row(norm_mix_g[layer]), w_in).reshape(b, s, -1)
            o_sb = _sb_attention(proj).reshape(b * s, SB_WIDTH)
            o_hg = _hgrn2(proj, hg_lb_logits.astype(F32), hg_norm_g[j].astype(F32),
                          layer=layer).reshape(b * s, HG_WIDTH)
            mix = (o_sb, o_hg, 0, 0)
            wp, bp = w_out_ab[j].astype(BF16), zeros_d
        else:
            y = _conv_front(h.reshape(b, s, d), row(norm_mix_g[layer]),
                            conv_w_glu[j].astype(BF16), row(conv_b_glu[j]),
                            conv_w_dw[j].astype(F32), row(conv_b_dw[j]), row(conv_ln_g[j]),
                            row(conv_ln_b[j])).reshape(b * s, d)
            mix = (y, y, 0, 1)
            wp, bp = conv_w_pw[j].astype(BF16), row(conv_b_pw[j])
        h = _proj_mlp(*mix, h, wp, bp, row(norm_ffn_g[layer]), w_ff1[layer].astype(BF16),
                      w_ff2[layer].astype(BF16), row(final_norm_g), final_norm=last)
    return h.reshape(b, s, d)
```

```python
import functools
import math

import jax
import jax.numpy as jnp
from jax import lax
from jax.experimental import pallas as pl
from jax.experimental.pallas import tpu as pltpu

F32 = jnp.float32
BF16 = jnp.bfloat16

RMS_EPS = 1e-6
LN_EPS = 1e-5

SB_HEADS = 8
SB_HEAD_DIM = 64
SB_WIDTH = SB_HEADS * SB_HEAD_DIM
HG_HEADS = 4
HG_HEAD_DIM = 128
HG_WIDTH = HG_HEADS * HG_HEAD_DIM
CONV_WIDTH = 31

LANES = 128
SUBLANES = 8
LOG2E = math.log2(math.e)
SB_BLOCK = 128
SB_HALF = SB_BLOCK // 2
SB_UNDERFLOW_LOG2 = -92.0 * LOG2E
HG_STEP = 32
HG_SUB = 256
CONV_HIST = 32
VMEM_LIMIT = 56 * 1024 * 1024

_NT = (((1,), (1,)), ((), ()))
_TN = (((0,), (0,)), ((), ()))


def _rms(x, g):
    return x * lax.rsqrt(jnp.mean(x * x, axis=-1, keepdims=True) + RMS_EPS) * g


def _dot(a, b):
    return jnp.dot(a, b, preferred_element_type=F32)


def _const_spec(shape):
    nd = len(shape)
    return pl.BlockSpec(shape, lambda *_: (0,) * nd, pipeline_mode=pl.Buffered(1))


def _inproj_kernel(x_ref, g_ref, w_ref, o_ref, *, tn):
    u = _rms(x_ref[...], g_ref[...]).astype(BF16)
    for n in range(w_ref.shape[1] // tn):
        cols = slice(n * tn, (n + 1) * tn)
        o_ref[:, cols] = _dot(u, w_ref[:, cols]).astype(BF16)


def _inproj(x2d, g, w, *, tm=512, tn=512):
    t, d = x2d.shape
    n = w.shape[1]
    tm = min(tm, t)
    return pl.pallas_call(
        functools.partial(_inproj_kernel, tn=tn),
        grid=(t // tm,),
        in_specs=[pl.BlockSpec((tm, d), lambda i: (i, 0)),
                  _const_spec((1, d)),
                  _const_spec((d, n))],
        out_specs=pl.BlockSpec((tm, n), lambda i: (i, 0)),
        out_shape=jax.ShapeDtypeStruct((t, n), BF16),
        compiler_params=pltpu.CompilerParams(dimension_semantics=("arbitrary",),
                                             vmem_limit_bytes=VMEM_LIMIT),
        name="inproj",
    )(x2d, g, w)


def _sb_kernel(q_ref, k_ref, v_ref, o_ref, r_ref, acc_ref):
    blk = SB_BLOCK
    n_sub = q_ref.shape[1] // blk
    qi0 = pl.program_id(1) * n_sub
    pairs = q_ref.shape[2] // LANES
    row = lax.broadcasted_iota(jnp.int32, (blk, blk), 0)
    col = lax.broadcasted_iota(jnp.int32, (blk, blk), 1)
    causal = col < row
    trow = lax.broadcasted_iota(jnp.int32, (blk, 2 * blk), 0)
    tcol = lax.broadcasted_iota(jnp.int32, (blk, 2 * blk), 1)
    tri_ones = jnp.where((trow > tcol) | (tcol >= blk), 1.0, 0.0).astype(BF16)

    def split(x):
        lo = lax.broadcasted_iota(jnp.int32, x.shape, 1) < SB_HEAD_DIM
        zero = jnp.zeros_like(x)
        return (jnp.where(lo, x, zero), jnp.where(lo, zero, x))

    def sweep(groups, fresh):
        units = [(p, groups) for p in range(pairs)]
        z_of, mid_of, vs_of = {}, {}, {}

        def stage_scores(u):
            p, grps = units[u]
            lanes = slice(p * LANES, (p + 1) * LANES)
            vs_of[u], z_of[u] = [], []
            for q_rows, windows, _ in grps:
                q2 = jnp.concatenate(split(q_ref[0, q_rows, lanes]), axis=0)
                vs_of[u].append([v_ref[0, w, lanes] for w in windows])
                z_of[u].append([lax.dot_general(q2, k_ref[0, w, lanes], _NT,
                                                preferred_element_type=F32) for w in windows])

        def stage_sums(u):
            _, grps = units[u]
            log_betas, log_keeps = [], []
            for gi, (_, windows, mask) in enumerate(grps):
                mask2 = None if mask is None else jnp.concatenate([mask, mask], axis=0)
                for j in range(len(windows)):
                    z = z_of[u][gi][j]
                    log_beta = jnp.minimum(z, 0.0) - jnp.log2(1.0 + jnp.exp2(-jnp.abs(z)))
                    log_keep = log_beta - z
                    if mask2 is not None and j == 0:
                        log_keep = jnp.where(mask2, log_keep, 0.0)
                    log_betas.append(log_beta)
                    log_keeps.append(log_keep.astype(BF16))
            sums = _dot(jnp.concatenate(log_keeps, axis=0), tri_ones)
            mid_of[u], at = [], 0
            for log_beta in log_betas:
                n = log_beta.shape[0]
                mid_of[u].append((log_beta, sums[at:at + n, :blk], sums[at:at + n, blk:]))
                at += n
            del z_of[u]

        def stage_values(u):
            p, grps = units[u]
            lanes = slice(p * LANES, (p + 1) * LANES)
            mids = iter(mid_of[u])
            for gi, (q_rows, windows, mask) in enumerate(grps):
                n = q_rows.stop - q_rows.start
                mask2 = None if mask is None else jnp.concatenate([mask, mask], axis=0)
                r = None if fresh else jnp.concatenate(
                    [r_ref[2 * p, q_rows, :], r_ref[2 * p + 1, q_rows, :]], axis=0)
                pv2 = None
                for j in range(len(windows)):
                    log_beta, later, total = next(mids)
                    e = log_beta + later
                    if r is not None:
                        e = e + r
                    w = jnp.exp2(e)
                    if mask2 is not None and j == 0:
                        w = jnp.where(mask2, w, 0.0)
                    r = total if r is None else r + total
                    part = _dot(w.astype(BF16), vs_of[u][gi][j])
                    pv2 = part if pv2 is None else pv2 + part
                r_ref[2 * p, q_rows, :] = r[:n]
                r_ref[2 * p + 1, q_rows, :] = r[n:]
                lo = lax.broadcasted_iota(jnp.int32, (n, LANES), 1) < SB_HEAD_DIM
                pv = jnp.where(lo, pv2[:n], pv2[n:])
                if fresh:
                    acc_ref[q_rows, lanes] = pv
                else:
                    acc_ref[q_rows, lanes] += pv
            del mid_of[u], vs_of[u]

        for t in range(len(units) + 2):
            if t < len(units):
                stage_scores(t)
            if 0 <= t - 1 < len(units):
                stage_sums(t - 1)
            if 0 <= t - 2 < len(units):
                stage_values(t - 2)

    def key_block(kb):
        return pl.ds(pl.multiple_of(kb * blk, blk), blk)

    def window_sweep():
        win = 2 * blk
        wr = lax.broadcasted_iota(jnp.int32, (win, win), 0)
        wc = lax.broadcasted_iota(jnp.int32, (win, win), 1)
        tri = jnp.where(wr > wc, 1.0, 0.0).astype(BF16)
        hr = lax.broadcasted_iota(jnp.int32, (2 * SB_HALF, blk), 0) % SB_HALF
        hc = lax.broadcasted_iota(jnp.int32, (2 * SB_HALF, blk), 1)
        newest_ok = hc < hr + SB_HALF
        halves = [(slice(first, first + SB_HALF),
                   pl.ds(pl.multiple_of(qi0 * blk + first + SB_HALF - win, SB_HALF), win))
                  for first in range(0, n_sub * blk, SB_HALF)]
        z_of, mid_of, v_of = {}, {}, {}
        worst = [jnp.full((1, 1), -jnp.inf, F32)]

        def masked(x):
            return jnp.concatenate([x[:, :blk], jnp.where(newest_ok, x[:, blk:], 0.0)], axis=1)

        def stage_scores(p):
            lanes = slice(p * LANES, (p + 1) * LANES)
            z_of[p], v_of[p] = [], []
            for q_rows, keys in halves:
                q2 = jnp.concatenate(split(q_ref[0, q_rows, lanes]), axis=0)
                v_of[p].append(v_ref[0, keys, lanes])
                z_of[p].append(lax.dot_general(q2, k_ref[0, keys, lanes], _NT,
                                               preferred_element_type=F32))

        def stage_sums(p):
            log_betas, log_keeps = [], []
            for z in z_of[p]:
                log_beta = jnp.minimum(z, 0.0) - jnp.log2(1.0 + jnp.exp2(-jnp.abs(z)))
                log_keep = masked(log_beta - z)
                log_betas.append(log_beta)
                log_keeps.append(log_keep)
            stacked = jnp.concatenate(log_keeps, axis=0)
            later = _dot(stacked.astype(BF16), tri)
            totals = later[:, 0:1] + stacked[:, 0:1]
            worst[0] = jnp.maximum(worst[0], jnp.max(totals, axis=0, keepdims=True))
            n = 2 * SB_HALF
            mid_of[p] = [(lb, later[i * n:(i + 1) * n]) for i, lb in enumerate(log_betas)]
            del z_of[p]

        def stage_values(p):
            lanes = slice(p * LANES, (p + 1) * LANES)
            lo = lax.broadcasted_iota(jnp.int32, (SB_HALF, LANES), 1) < SB_HEAD_DIM
            for (q_rows, _), (log_beta, later), v in zip(halves, mid_of[p], v_of[p]):
                w = masked(jnp.exp2(log_beta + later))
                pv2 = _dot(w.astype(BF16), v)
                acc_ref[q_rows, lanes] = jnp.where(lo, pv2[:SB_HALF], pv2[SB_HALF:])
            del mid_of[p], v_of[p]

        for t in range(pairs + 2):
            if t < pairs:
                stage_scores(t)
            if 0 <= t - 1 < pairs:
                stage_sums(t - 1)
            if 0 <= t - 2 < pairs:
                stage_values(t - 2)
        r_ref[0, 0:SUBLANES, :] = jnp.broadcast_to(worst[0], (SUBLANES, blk))

    @pl.when(qi0 < 2)
    def _():
        r_ref[0, 0:SUBLANES, :] = jnp.zeros((SUBLANES, blk), F32)

    @pl.when(qi0 >= 2)
    def _():
        window_sweep()

    redo = jnp.max(r_ref[0, 0:SUBLANES, :]) > SB_UNDERFLOW_LOG2

    @pl.when(redo)
    def _():
        for sub in range(n_sub):
            q_rows = slice(sub * blk, (sub + 1) * blk)
            sweep([(q_rows, [key_block(qi0 + sub)], causal)], True)

            def cond(c):
                kb, m = c
                return jnp.logical_and(kb >= 0, m > SB_UNDERFLOW_LOG2)

            def body(c, q_rows=q_rows):
                kb, _ = c
                sweep([(q_rows, [key_block(kb)], None)], False)
                return kb - 1, jnp.max(r_ref[:, q_rows, :])

            lax.while_loop(cond, body, (qi0 + sub - 1, jnp.max(r_ref[:, q_rows, :])))

    o_ref[0] = acc_ref[...].astype(BF16)


def _sb_attention(proj3, *, q_rows=2 * SB_BLOCK):
    b, s, _ = proj3.shape
    q_rows = min(q_rows, s)
    kv_spec = lambda c: pl.BlockSpec((1, s, SB_WIDTH), lambda bi, qi: (bi, 0, c),
                                     pipeline_mode=pl.Buffered(1))
    return pl.pallas_call(
        _sb_kernel,
        grid=(b, s // q_rows),
        in_specs=[pl.BlockSpec((1, q_rows, SB_WIDTH), lambda bi, qi: (bi, qi, 0)),
                  kv_spec(1), kv_spec(2)],
        out_specs=pl.BlockSpec((1, q_rows, SB_WIDTH), lambda bi, qi: (bi, qi, 0)),
        out_shape=jax.ShapeDtypeStruct((b, s, SB_WIDTH), BF16),
        scratch_shapes=[pltpu.VMEM((SB_HEADS, q_rows, SB_BLOCK), F32),
                        pltpu.VMEM((q_rows, SB_WIDTH), F32)],
        compiler_params=pltpu.CompilerParams(dimension_semantics=("arbitrary", "arbitrary"),
                                             vmem_limit_bytes=VMEM_LIMIT),
        name="sb_attention",
    )(proj3, proj3, proj3)


def _hg_kernel(q_ref, f_ref, i_ref, gate_ref, lbl_ref, ng_ref, o_ref, state_ref, *, layer):
    si = pl.program_id(2)
    step, sub = HG_STEP, HG_SUB
    half = step // 2

    @pl.when(si == 0)
    def _():
        state_ref[...] = jnp.zeros_like(state_ref)

    logits = lbl_ref[...]
    e = jnp.exp(logits - jnp.max(logits, axis=0, keepdims=True))
    lb = jnp.sum(e[:layer + 1], axis=0, keepdims=True) / jnp.sum(e, axis=0, keepdims=True)
    ng = ng_ref[0]

    r = lax.broadcasted_iota(jnp.int32, (sub, sub), 0)
    c = lax.broadcasted_iota(jnp.int32, (sub, sub), 1)
    same = (r // step) == (c // step)
    incl = jnp.where(same & (c <= r), 1.0, 0.0)
    to_mid = jnp.where(same & (c % step < half), 1.0, 0.0)
    m_q = (incl - to_mid).astype(BF16)
    m_k = jnp.where(same & (c > r), 1.0, 0.0).astype(BF16)
    m_g = incl.astype(BF16)
    tr = lax.broadcasted_iota(jnp.int32, (step, step), 0)
    tc = lax.broadcasted_iota(jnp.int32, (step, step), 1)
    tril = tr >= tc
    per_grp = LANES // step
    lane_step = lax.broadcasted_iota(jnp.int32, (HG_HEAD_DIM, LANES), 1) // step

    def body(j, carry):
        rows = pl.ds(pl.multiple_of(j * sub, sub), sub)
        q = q_ref[0, rows, :].astype(F32)
        f = lb + (1.0 - lb) * jax.nn.sigmoid(f_ref[0, rows, :].astype(F32))
        v = i_ref[0, rows, :]
        g = jnp.log(f)
        kk = 1.0 - f
        g_hi = g.astype(BF16)
        g_lo = (g - g_hi.astype(F32)).astype(BF16)
        d_q = _dot(m_q, g_hi) + _dot(m_q, g_lo)
        d_k = _dot(m_k, g_hi) + _dot(m_k, g_lo)
        e_g = jnp.exp(_dot(m_g, g_hi) + _dot(m_g, g_lo))
        q_intra = (q * jnp.exp(d_q)).astype(BF16)
        k_intra = (kk * jnp.exp(-d_q)).astype(BF16)
        k_state = (kk * jnp.exp(d_k)).astype(BF16)
        q_inter = (q * e_g).astype(BF16)
        steps = [slice(n * step, (n + 1) * step) for n in range(sub // step)]
        scores = [lax.dot_general(q_intra[sl], k_intra[sl], _NT, preferred_element_type=F32)
                  for sl in steps]
        upd_t = [lax.dot_general(v[sl], k_state[sl], _TN, preferred_element_type=F32)
                 for sl in steps]
        intra = [_dot(jnp.where(tril, sc, 0.0).astype(BF16), v[sl])
                 for sc, sl in zip(scores, steps)]
        state_t = state_ref[...]
        inter = []
        for gi in range(sub // LANES):
            q_grp = q_inter[gi * LANES:(gi + 1) * LANES]
            o_t = None
            for m in range(per_grp):
                n = gi * per_grp + m
                res = lax.dot_general(state_t.astype(BF16), q_grp, _NT, preferred_element_type=F32)
                o_t = res if o_t is None else jnp.where(lane_step == m, res, o_t)
                state_t = state_t * e_g[(n + 1) * step - 1:(n + 1) * step] + upd_t[n]
            inter.append(o_t.T)
        state_ref[...] = state_t
        o = jnp.concatenate(intra, axis=0) + jnp.concatenate(inter, axis=0)
        o = o * lax.rsqrt(jnp.mean(o * o, axis=-1, keepdims=True) + RMS_EPS) * ng
        o = o * jax.nn.silu(gate_ref[0, rows, :].astype(F32))
        o_ref[0, rows, :] = o.astype(BF16)
        return carry

    lax.fori_loop(0, q_ref.shape[1] // sub, body, 0, unroll=True)


def _hgrn2(proj3, lb_logits, norm_g, *, layer, ts=2048):
    b, s, _ = proj3.shape
    ts = min(ts, s)
    base = 3 * SB_WIDTH // LANES
    depth1 = lb_logits.shape[0]

    def col(k):
        return pl.BlockSpec((1, ts, LANES), lambda bi, h, si: (bi, si, base + k * HG_HEADS + h))

    return pl.pallas_call(
        functools.partial(_hg_kernel, layer=layer),
        grid=(b, HG_HEADS, s // ts),
        in_specs=[col(0), col(1), col(2), col(3),
                  pl.BlockSpec((depth1, LANES), lambda bi, h, si: (0, h)),
                  pl.BlockSpec((1, 1, LANES), lambda bi, h, si: (h, 0, 0))],
        out_specs=pl.BlockSpec((1, ts, LANES), lambda bi, h, si: (bi, si, h)),
        out_shape=jax.ShapeDtypeStruct((b, s, HG_WIDTH), BF16),
        scratch_shapes=[pltpu.VMEM((HG_HEAD_DIM, HG_HEAD_DIM), F32)],
        compiler_params=pltpu.CompilerParams(
            dimension_semantics=("arbitrary", "arbitrary", "arbitrary"),
            vmem_limit_bytes=VMEM_LIMIT),
        name="hgrn2",
    )(proj3, proj3, proj3, proj3, lb_logits, norm_g.reshape(HG_HEADS, 1, HG_HEAD_DIM))


def _proj_mlp_kernel(ma_ref, mb_ref, h_ref, wp_ref, bp_ref, g_ref, w1_ref, w2_ref, gf_ref,
                     o_ref, *, tf, final_norm):
    half = ma_ref.shape[1]
    h1 = (h_ref[...] + _dot(ma_ref[...], wp_ref[:half, :]) + _dot(mb_ref[...], wp_ref[half:, :])
          + bp_ref[...])
    u = _rms(h1, g_ref[...]).astype(BF16)
    acc = h1
    for c in range(w1_ref.shape[1] // tf):
        a = jnp.maximum(_dot(u, w1_ref[:, c * tf:(c + 1) * tf]), 0.0)
        acc = acc + _dot((a * a).astype(BF16), w2_ref[c * tf:(c + 1) * tf, :])
    if final_norm:
        acc = _rms(acc, gf_ref[...])
    o_ref[...] = acc


def _proj_mlp(ma, mb, cols_a, cols_b, h2d, wp, bp, g, w1, w2, gf, *, final_norm, tm=512, tf=1024):
    t, d = h2d.shape
    half = d // 2
    dff = w1.shape[1]
    tm = min(tm, t)
    return pl.pallas_call(
        functools.partial(_proj_mlp_kernel, tf=tf, final_norm=final_norm),
        grid=(t // tm,),
        in_specs=[pl.BlockSpec((tm, half), lambda i: (i, cols_a)),
                  pl.BlockSpec((tm, half), lambda i: (i, cols_b)),
                  pl.BlockSpec((tm, d), lambda i: (i, 0)),
                  _const_spec((d, d)), _const_spec((1, d)), _const_spec((1, d)),
                  _const_spec((d, dff)), _const_spec((dff, d)), _const_spec((1, d))],
        out_specs=pl.BlockSpec((tm, d), lambda i: (i, 0)),
        out_shape=jax.ShapeDtypeStruct((t, d), F32),
        compiler_params=pltpu.CompilerParams(dimension_semantics=("arbitrary",),
                                             vmem_limit_bytes=VMEM_LIMIT),
        name="proj_mlp_final" if final_norm else "proj_mlp",
    )(ma, mb, h2d, wp, bp, g, w1, w2, gf)


def _conv_kernel(h_ref, g_ref, wg_ref, bg_ref, wdw_ref, bdw_ref, lng_ref, lnb_ref, o_ref,
                 abuf_ref, y_ref, *, rc):
    si = pl.program_id(1)
    ts, d = h_ref.shape[1], h_ref.shape[2]
    hist = CONV_HIST

    @pl.when(si == 0)
    def _():
        abuf_ref[0:hist, :] = jnp.zeros((hist, d), F32)

    @pl.when(si > 0)
    def _():
        abuf_ref[0:hist, :] = abuf_ref[ts:ts + hist, :]

    u = _rms(h_ref[0], g_ref[...]).astype(BF16)
    lin_a = _dot(u, wg_ref[:, :d]) + bg_ref[:, :d]
    lin_b = _dot(u, wg_ref[:, d:]) + bg_ref[:, d:]
    abuf_ref[hist:, :] = lin_a * jax.nn.sigmoid(lin_b)

    for l0 in range(0, d, LANES):
        lanes = slice(l0, l0 + LANES)
        for r0 in range(0, ts, rc):
            win = abuf_ref[r0:r0 + rc + hist, lanes]
            acc = jnp.broadcast_to(bdw_ref[:, lanes], (rc, LANES))
            for sft in range(SUBLANES):
                rolled = win if sft == 0 else pltpu.roll(win, sft, axis=0)
                for dly in range(sft, CONV_WIDTH, SUBLANES):
                    tap = CONV_WIDTH - 1 - dly
                    start = hist - (dly - sft)
                    acc = acc + wdw_ref[tap:tap + 1, lanes] * rolled[start:start + rc]
            y_ref[r0:r0 + rc, lanes] = acc

    y = y_ref[...]
    mu = jnp.mean(y, axis=-1, keepdims=True)
    yc = y - mu
    var = jnp.mean(yc * yc, axis=-1, keepdims=True)
    yn = yc * lax.rsqrt(var + LN_EPS) * lng_ref[...] + lnb_ref[...]
    o_ref[0] = (yn * jax.nn.sigmoid(yn)).astype(BF16)


def _conv_front(h3, g, wg, bg, wdw, bdw, lng, lnb, *, ts=512, rc=64):
    b, s, d = h3.shape
    ts = min(ts, s)
    return pl.pallas_call(
        functools.partial(_conv_kernel, rc=rc),
        grid=(b, s // ts),
        in_specs=[pl.BlockSpec((1, ts, d), lambda bi, si: (bi, si, 0)),
                  _const_spec((1, d)), _const_spec((d, 2 * d)), _const_spec((1, 2 * d)),
                  _const_spec((CONV_WIDTH, d)), _const_spec((1, d)), _const_spec((1, d)),
                  _const_spec((1, d))],
        out_specs=pl.BlockSpec((1, ts, d), lambda bi, si: (bi, si, 0)),
        out_shape=jax.ShapeDtypeStruct((b, s, d), BF16),
        scratch_shapes=[pltpu.VMEM((ts + CONV_HIST, d), F32), pltpu.VMEM((ts, d), F32)],
        compiler_params=pltpu.CompilerParams(dimension_semantics=("arbitrary", "arbitrary"),
                                             vmem_limit_bytes=VMEM_LIMIT),
        name="conv_front",
    )(h3, g, wg, bg, wdw, bdw, lng, lnb)


def kernel(x, norm_mix_g, norm_ffn_g, w_in_ab, w_out_ab, hg_lb_logits, hg_norm_g, conv_w_glu,
           conv_b_glu, conv_w_dw, conv_b_dw, conv_ln_g, conv_ln_b, conv_w_pw, conv_b_pw,
           w_ff1, w_ff2, final_norm_g):
    b, s, d = x.shape
    depth = norm_mix_g.shape[0]
    row = lambda a: a.reshape(1, -1).astype(F32)
    zeros_d = jnp.zeros((1, d), F32)
    h = x.reshape(b * s, d)
    for layer in range(depth):
        j = layer // 2
        last = layer == depth - 1
        if layer % 2 == 0:
            q_scale = SB_HEAD_DIM ** -0.5 * LOG2E
            scale = jnp.where(jnp.arange(w_in_ab.shape[2]) < SB_WIDTH, q_scale, 1.0)
            w_in = (w_in_ab[j] * scale).astype(BF16)
            proj = _inproj(h, row(norm_mix_g[layer]), w_in).reshape(b, s, -1)
            o_sb = _sb_attention(proj).reshape(b * s, SB_WIDTH)
            o_hg = _hgrn2(proj, hg_lb_logits.astype(F32), hg_norm_g[j].astype(F32),
                          layer=layer).reshape(b * s, HG_WIDTH)
            mix = (o_sb, o_hg, 0, 0)
            wp, bp = w_out_ab[j].astype(BF16), zeros_d
        else:
            y = _conv_front(h.reshape(b, s, d), row(norm_mix_g[layer]),
                            conv_w_glu[j].astype(BF16), row(conv_b_glu[j]),
                            conv_w_dw[j].astype(F32), row(conv_b_dw[j]), row(conv_ln_g[j]),
                            row(conv_ln_b[j])).reshape(b * s, d)
            mix = (y, y, 0, 1)
            wp, bp = conv_w_pw[j].astype(BF16), row(conv_b_pw[j])
        h = _proj_mlp(*mix, h, wp, bp, row(norm_ffn_g[layer]), w_ff1[layer].astype(BF16),
                      w_ff2[layer].astype(BF16), row(final_norm_g), final_norm=last)
    return h.reshape(b, s, d)
```

```python
import functools
import math

import jax
import jax.numpy as jnp
from jax import lax
from jax.experimental import pallas as pl
from jax.experimental.pallas import tpu as pltpu

F32 = jnp.float32
BF16 = jnp.bfloat16

RMS_EPS = 1e-6
LN_EPS = 1e-5

SB_HEADS = 8
SB_HEAD_DIM = 64
SB_WIDTH = SB_HEADS * SB_HEAD_DIM
HG_HEADS = 4
HG_HEAD_DIM = 128
HG_WIDTH = HG_HEADS * HG_HEAD_DIM
CONV_WIDTH = 31

LANES = 128
SUBLANES = 8
LOG2E = math.log2(math.e)
SB_BLOCK = 128
SB_HALF = SB_BLOCK // 2
SB_UNDERFLOW_LOG2 = -92.0 * LOG2E
HG_STEP = 32
HG_SUB = 256
CONV_HIST = 32
VMEM_LIMIT = 56 * 1024 * 1024

_NT = (((1,), (1,)), ((), ()))
_TN = (((0,), (0,)), ((), ()))


def _rms(x, g):
    return x * lax.rsqrt(jnp.mean(x * x, axis=-1, keepdims=True) + RMS_EPS) * g


def _dot(a, b):
    return jnp.dot(a, b, preferred_element_type=F32)


def _const_spec(shape):
    nd = len(shape)
    return pl.BlockSpec(shape, lambda *_: (0,) * nd, pipeline_mode=pl.Buffered(1))


def _inproj_kernel(x_ref, g_ref, w_ref, o_ref, *, tn):
    u = _rms(x_ref[...], g_ref[...]).astype(BF16)
    for n in range(w_ref.shape[1] // tn):
        cols = slice(n * tn, (n + 1) * tn)
        o_ref[:, cols] = _dot(u, w_ref[:, cols]).astype(BF16)


def _inproj(x2d, g, w, *, tm=1024, tn=512):
    t, d = x2d.shape
    n = w.shape[1]
    tm = min(tm, t)
    return pl.pallas_call(
        functools.partial(_inproj_kernel, tn=tn),
        grid=(t // tm,),
        in_specs=[pl.BlockSpec((tm, d), lambda i: (i, 0)),
                  _const_spec((1, d)),
                  _const_spec((d, n))],
        out_specs=pl.BlockSpec((tm, n), lambda i: (i, 0)),
        out_shape=jax.ShapeDtypeStruct((t, n), BF16),
        compiler_params=pltpu.CompilerParams(dimension_semantics=("arbitrary",),
                                             vmem_limit_bytes=VMEM_LIMIT),
        name="inproj",
    )(x2d, g, w)


def _sb_kernel(q_ref, k_ref, v_ref, o_ref, r_ref, acc_ref):
    blk = SB_BLOCK
    n_sub = q_ref.shape[1] // blk
    qi0 = pl.program_id(1) * n_sub
    pairs = q_ref.shape[2] // LANES
    row = lax.broadcasted_iota(jnp.int32, (blk, blk), 0)
    col = lax.broadcasted_iota(jnp.int32, (blk, blk), 1)
    causal = col < row
    trow = lax.broadcasted_iota(jnp.int32, (blk, 2 * blk), 0)
    tcol = lax.broadcasted_iota(jnp.int32, (blk, 2 * blk), 1)
    tri_ones = jnp.where((trow > tcol) | (tcol >= blk), 1.0, 0.0).astype(BF16)

    def split(x):
        lo = lax.broadcasted_iota(jnp.int32, x.shape, 1) < SB_HEAD_DIM
        zero = jnp.zeros_like(x)
        return (jnp.where(lo, x, zero), jnp.where(lo, zero, x))

    def sweep(groups, fresh):
        units = [(p, groups) for p in range(pairs)]
        z_of, mid_of, vs_of = {}, {}, {}

        def stage_scores(u):
            p, grps = units[u]
            lanes = slice(p * LANES, (p + 1) * LANES)
            vs_of[u], z_of[u] = [], []
            for q_rows, windows, _ in grps:
                q2 = jnp.concatenate(split(q_ref[0, q_rows, lanes]), axis=0)
                vs_of[u].append([v_ref[0, w, lanes] for w in windows])
                z_of[u].append([lax.dot_general(q2, k_ref[0, w, lanes], _NT,
                                                preferred_element_type=F32) for w in windows])

        def stage_sums(u):
            _, grps = units[u]
            log_betas, log_keeps = [], []
            for gi, (_, windows, mask) in enumerate(grps):
                mask2 = None if mask is None else jnp.concatenate([mask, mask], axis=0)
                for j in range(len(windows)):
                    z = z_of[u][gi][j]
                    log_beta = jnp.minimum(z, 0.0) - jnp.log2(1.0 + jnp.exp2(-jnp.abs(z)))
                    log_keep = log_beta - z
                    if mask2 is not None and j == 0:
                        log_keep = jnp.where(mask2, log_keep, 0.0)
                    log_betas.append(log_beta)
                    log_keeps.append(log_keep.astype(BF16))
            sums = _dot(jnp.concatenate(log_keeps, axis=0), tri_ones)
            mid_of[u], at = [], 0
            for log_beta in log_betas:
                n = log_beta.shape[0]
                mid_of[u].append((log_beta, sums[at:at + n, :blk], sums[at:at + n, blk:]))
                at += n
            del z_of[u]

        def stage_values(u):
            p, grps = units[u]
            lanes = slice(p * LANES, (p + 1) * LANES)
            mids = iter(mid_of[u])
            for gi, (q_rows, windows, mask) in enumerate(grps):
                n = q_rows.stop - q_rows.start
                mask2 = None if mask is None else jnp.concatenate([mask, mask], axis=0)
                r = None if fresh else jnp.concatenate(
                    [r_ref[2 * p, q_rows, :], r_ref[2 * p + 1, q_rows, :]], axis=0)
                pv2 = None
                for j in range(len(windows)):
                    log_beta, later, total = next(mids)
                    e = log_beta + later
                    if r is not None:
                        e = e + r
                    w = jnp.exp2(e)
                    if mask2 is not None and j == 0:
                        w = jnp.where(mask2, w, 0.0)
                    r = total if r is None else r + total
                    part = _dot(w.astype(BF16), vs_of[u][gi][j])
                    pv2 = part if pv2 is None else pv2 + part
                r_ref[2 * p, q_rows, :] = r[:n]
                r_ref[2 * p + 1, q_rows, :] = r[n:]
                lo = lax.broadcasted_iota(jnp.int32, (n, LANES), 1) < SB_HEAD_DIM
                pv = jnp.where(lo, pv2[:n], pv2[n:])
                if fresh:
                    acc_ref[q_rows, lanes] = pv
                else:
                    acc_ref[q_rows, lanes] += pv
            del mid_of[u], vs_of[u]

        for t in range(len(units) + 2):
            if t < len(units):
                stage_scores(t)
            if 0 <= t - 1 < len(units):
                stage_sums(t - 1)
            if 0 <= t - 2 < len(units):
                stage_values(t - 2)

    def key_block(kb):
        return pl.ds(pl.multiple_of(kb * blk, blk), blk)

    def window_sweep():
        win = 2 * blk
        wr = lax.broadcasted_iota(jnp.int32, (win, win), 0)
        wc = lax.broadcasted_iota(jnp.int32, (win, win), 1)
        tri = jnp.where(wr > wc, 1.0, 0.0).astype(BF16)
        hr = lax.broadcasted_iota(jnp.int32, (2 * SB_HALF, blk), 0) % SB_HALF
        hc = lax.broadcasted_iota(jnp.int32, (2 * SB_HALF, blk), 1)
        newest_ok = hc < hr + SB_HALF
        halves = [(slice(first, first + SB_HALF),
                   pl.ds(pl.multiple_of(qi0 * blk + first + SB_HALF - win, SB_HALF), win))
                  for first in range(0, n_sub * blk, SB_HALF)]
        z_of, mid_of, v_of = {}, {}, {}
        worst = [jnp.full((1, 1), -jnp.inf, F32)]

        def masked(x):
            return jnp.concatenate([x[:, :blk], jnp.where(newest_ok, x[:, blk:], 0.0)], axis=1)

        def stage_scores(p):
            lanes = slice(p * LANES, (p + 1) * LANES)
            z_of[p], v_of[p] = [], []
            for q_rows, keys in halves:
                q2 = jnp.concatenate(split(q_ref[0, q_rows, lanes]), axis=0)
                v_of[p].append(v_ref[0, keys, lanes])
                z_of[p].append(lax.dot_general(q2, k_ref[0, keys, lanes], _NT,
                                               preferred_element_type=F32))

        def stage_sums(p):
            log_betas, log_keeps = [], []
            for z in z_of[p]:
                log_beta = jnp.minimum(z, 0.0) - jnp.log2(1.0 + jnp.exp2(-jnp.abs(z)))
                log_keep = masked(log_beta - z)
                log_betas.append(log_beta)
                log_keeps.append(log_keep)
            stacked = jnp.concatenate(log_keeps, axis=0)
            later = _dot(stacked.astype(BF16), tri)
            totals = later[:, 0:1] + stacked[:, 0:1]
            worst[0] = jnp.maximum(worst[0], jnp.max(totals, axis=0, keepdims=True))
            n = 2 * SB_HALF
            mid_of[p] = [(lb, later[i * n:(i + 1) * n]) for i, lb in enumerate(log_betas)]
            del z_of[p]

        def stage_values(p):
            lanes = slice(p * LANES, (p + 1) * LANES)
            lo = lax.broadcasted_iota(jnp.int32, (SB_HALF, LANES), 1) < SB_HEAD_DIM
            for (q_rows, _), (log_beta, later), v in zip(halves, mid_of[p], v_of[p]):
                w = masked(jnp.exp2(log_beta + later))
                pv2 = _dot(w.astype(BF16), v)
                acc_ref[q_rows, lanes] = jnp.where(lo, pv2[:SB_HALF], pv2[SB_HALF:])
            del mid_of[p], v_of[p]

        for t in range(pairs + 2):
            if t < pairs:
                stage_scores(t)
            if 0 <= t - 1 < pairs:
                stage_sums(t - 1)
            if 0 <= t - 2 < pairs:
                stage_values(t - 2)
        r_ref[0, 0:SUBLANES, :] = jnp.broadcast_to(worst[0], (SUBLANES, blk))

    @pl.when(qi0 < 2)
    def _():
        r_ref[0, 0:SUBLANES, :] = jnp.zeros((SUBLANES, blk), F32)

    @pl.when(qi0 >= 2)
    def _():
        window_sweep()

    redo = jnp.max(r_ref[0, 0:SUBLANES, :]) > SB_UNDERFLOW_LOG2

    @pl.when(redo)
    def _():
        for sub in range(n_sub):
            q_rows = slice(sub * blk, (sub + 1) * blk)
            sweep([(q_rows, [key_block(qi0 + sub)], causal)], True)

            def cond(c):
                kb, m = c
                return jnp.logical_and(kb >= 0, m > SB_UNDERFLOW_LOG2)

            def body(c, q_rows=q_rows):
                kb, _ = c
                sweep([(q_rows, [key_block(kb)], None)], False)
                return kb - 1, jnp.max(r_ref[:, q_rows, :])

            lax.while_loop(cond, body, (qi0 + sub - 1, jnp.max(r_ref[:, q_rows, :])))

    o_ref[0] = acc_ref[...].astype(BF16)


def _sb_attention(proj3, *, q_rows=2 * SB_BLOCK):
    b, s, _ = proj3.shape
    q_rows = min(q_rows, s)
    kv_spec = lambda c: pl.BlockSpec((1, s, SB_WIDTH), lambda bi, qi: (bi, 0, c),
                                     pipeline_mode=pl.Buffered(1))
    return pl.pallas_call(
        _sb_kernel,
        grid=(b, s // q_rows),
        in_specs=[pl.BlockSpec((1, q_rows, SB_WIDTH), lambda bi, qi: (bi, qi, 0)),
                  kv_spec(1), kv_spec(2)],
        out_specs=pl.BlockSpec((1, q_rows, SB_WIDTH), lambda bi, qi: (bi, qi, 0)),
        out_shape=jax.ShapeDtypeStruct((b, s, SB_WIDTH), BF16),
        scratch_shapes=[pltpu.VMEM((SB_HEADS, q_rows, SB_BLOCK), F32),
                        pltpu.VMEM((q_rows, SB_WIDTH), F32)],
        compiler_params=pltpu.CompilerParams(dimension_semantics=("arbitrary", "arbitrary"),
                                             vmem_limit_bytes=VMEM_LIMIT),
        name="sb_attention",
    )(proj3, proj3, proj3)


def _hg_kernel(q_ref, f_ref, i_ref, gate_ref, lbl_ref, ng_ref, o_ref, state_ref, *, layer):
    si = pl.program_id(2)
    step, sub = HG_STEP, HG_SUB
    half = step // 2

    @pl.when(si == 0)
    def _():
        state_ref[...] = jnp.zeros_like(state_ref)

    logits = lbl_ref[...]
    e = jnp.exp(logits - jnp.max(logits, axis=0, keepdims=True))
    lb = jnp.sum(e[:layer + 1], axis=0, keepdims=True) / jnp.sum(e, axis=0, keepdims=True)
    ng = ng_ref[0]

    r = lax.broadcasted_iota(jnp.int32, (sub, sub), 0)
    c = lax.broadcasted_iota(jnp.int32, (sub, sub), 1)
    same = (r // step) == (c // step)
    incl = jnp.where(same & (c <= r), 1.0, 0.0)
    to_mid = jnp.where(same & (c % step < half), 1.0, 0.0)
    m_q = (incl - to_mid).astype(BF16)
    m_k = jnp.where(same & (c > r), 1.0, 0.0).astype(BF16)
    m_g = incl.astype(BF16)
    tr = lax.broadcasted_iota(jnp.int32, (step, step), 0)
    tc = lax.broadcasted_iota(jnp.int32, (step, step), 1)
    tril = tr >= tc
    per_grp = LANES // step
    lane_step = lax.broadcasted_iota(jnp.int32, (HG_HEAD_DIM, LANES), 1) // step

    def body(j, carry):
        rows = pl.ds(pl.multiple_of(j * sub, sub), sub)
        q = q_ref[0, rows, :].astype(F32)
        f = lb + (1.0 - lb) * jax.nn.sigmoid(f_ref[0, rows, :].astype(F32))
        v = i_ref[0, rows, :]
        g = jnp.log(f)
        kk = 1.0 - f
        g_hi = g.astype(BF16)
        g_lo = (g - g_hi.astype(F32)).astype(BF16)
        d_q = _dot(m_q, g_hi) + _dot(m_q, g_lo)
        d_k = _dot(m_k, g_hi) + _dot(m_k, g_lo)
        e_g = jnp.exp(_dot(m_g, g_hi) + _dot(m_g, g_lo))
        q_intra = (q * jnp.exp(d_q)).astype(BF16)
        k_intra = (kk * jnp.exp(-d_q)).astype(BF16)
        k_state = (kk * jnp.exp(d_k)).astype(BF16)
        q_inter = (q * e_g).astype(BF16)
        steps = [slice(n * step, (n + 1) * step) for n in range(sub // step)]
        scores = [lax.dot_general(q_intra[sl], k_intra[sl], _NT, preferred_element_type=F32)
                  for sl in steps]
        upd_t = [lax.dot_general(v[sl], k_state[sl], _TN, preferred_element_type=F32)
                 for sl in steps]
        intra = [_dot(jnp.where(tril, sc, 0.0).astype(BF16), v[sl])
                 for sc, sl in zip(scores, steps)]
        state_t = state_ref[...]
        inter = []
        for gi in range(sub // LANES):
            q_grp = q_inter[gi * LANES:(gi + 1) * LANES]
            o_t = None
            for m in range(per_grp):
                n = gi * per_grp + m
                res = lax.dot_general(state_t.astype(BF16), q_grp, _NT, preferred_element_type=F32)
                o_t = res if o_t is None else jnp.where(lane_step == m, res, o_t)
                state_t = state_t * e_g[(n + 1) * step - 1:(n + 1) * step] + upd_t[n]
            inter.append(o_t.T)
        state_ref[...] = state_t
        o = jnp.concatenate(intra, axis=0) + jnp.concatenate(inter, axis=0)
        o = o * lax.rsqrt(jnp.mean(o * o, axis=-1, keepdims=True) + RMS_EPS) * ng
        o = o * jax.nn.silu(gate_ref[0, rows, :].astype(F32))
        o_ref[0, rows, :] = o.astype(BF16)
        return carry

    lax.fori_loop(0, q_ref.shape[1] // sub, body, 0, unroll=True)


def _hgrn2(proj3, lb_logits, norm_g, *, layer, ts=2048):
    b, s, _ = proj3.shape
    ts = min(ts, s)
    base = 3 * SB_WIDTH // LANES
    depth1 = lb_logits.shape[0]

    def col(k):
        return pl.BlockSpec((1, ts, LANES), lambda bi, h, si: (bi, si, base + k * HG_HEADS + h))

    return pl.pallas_call(
        functools.partial(_hg_kernel, layer=layer),
        grid=(b, HG_HEADS, s // ts),
        in_specs=[col(0), col(1), col(2), col(3),
                  pl.BlockSpec((depth1, LANES), lambda bi, h, si: (0, h)),
                  pl.BlockSpec((1, 1, LANES), lambda bi, h, si: (h, 0, 0))],
        out_specs=pl.BlockSpec((1, ts, LANES), lambda bi, h, si: (bi, si, h)),
        out_shape=jax.ShapeDtypeStruct((b, s, HG_WIDTH), BF16),
        scratch_shapes=[pltpu.VMEM((HG_HEAD_DIM, HG_HEAD_DIM), F32)],
        compiler_params=pltpu.CompilerParams(
            dimension_semantics=("arbitrary", "arbitrary", "arbitrary"),
            vmem_limit_bytes=VMEM_LIMIT),
        name="hgrn2",
    )(proj3, proj3, proj3, proj3, lb_logits, norm_g.reshape(HG_HEADS, 1, HG_HEAD_DIM))


def _proj_mlp_kernel(ma_ref, mb_ref, h_ref, wp_ref, bp_ref, g_ref, w1_ref, w2_ref, gf_ref,
                     o_ref, *, tf, final_norm):
    half = ma_ref.shape[1]
    h1 = (h_ref[...] + _dot(ma_ref[...], wp_ref[:half, :]) + _dot(mb_ref[...], wp_ref[half:, :])
          + bp_ref[...])
    u = _rms(h1, g_ref[...]).astype(BF16)
    acc = h1
    for c in range(w1_ref.shape[1] // tf):
        a = jnp.maximum(_dot(u, w1_ref[:, c * tf:(c + 1) * tf]), 0.0)
        acc = acc + _dot((a * a).astype(BF16), w2_ref[c * tf:(c + 1) * tf, :])
    if final_norm:
        acc = _rms(acc, gf_ref[...])
    o_ref[...] = acc


def _proj_mlp(ma, mb, cols_a, cols_b, h2d, wp, bp, g, w1, w2, gf, *, final_norm, tm=1024, tf=1024):
    t, d = h2d.shape
    half = d // 2
    dff = w1.shape[1]
    tm = min(tm, t)
    return pl.pallas_call(
        functools.partial(_proj_mlp_kernel, tf=tf, final_norm=final_norm),
        grid=(t // tm,),
        in_specs=[pl.BlockSpec((tm, half), lambda i: (i, cols_a)),
                  pl.BlockSpec((tm, half), lambda i: (i, cols_b)),
                  pl.BlockSpec((tm, d), lambda i: (i, 0)),
                  _const_spec((d, d)), _const_spec((1, d)), _const_spec((1, d)),
                  _const_spec((d, dff)), _const_spec((dff, d)), _const_spec((1, d))],
        out_specs=pl.BlockSpec((tm, d), lambda i: (i, 0)),
        out_shape=jax.ShapeDtypeStruct((t, d), F32),
        compiler_params=pltpu.CompilerParams(dimension_semantics=("arbitrary",),
                                             vmem_limit_bytes=VMEM_LIMIT),
        name="proj_mlp_final" if final_norm else "proj_mlp",
    )(ma, mb, h2d, wp, bp, g, w1, w2, gf)


def _conv_kernel(h_ref, g_ref, wg_ref, bg_ref, wdw_ref, bdw_ref, lng_ref, lnb_ref, o_ref,
                 abuf_ref, y_ref, *, rc):
    si = pl.program_id(1)
    ts, d = h_ref.shape[1], h_ref.shape[2]
    hist = CONV_HIST

    @pl.when(si == 0)
    def _():
        abuf_ref[0:hist, :] = jnp.zeros((hist, d), F32)

    @pl.when(si > 0)
    def _():
        abuf_ref[0:hist, :] = abuf_ref[ts:ts + hist, :]

    u = _rms(h_ref[0], g_ref[...]).astype(BF16)
    lin_a = _dot(u, wg_ref[:, :d]) + bg_ref[:, :d]
    lin_b = _dot(u, wg_ref[:, d:]) + bg_ref[:, d:]
    abuf_ref[hist:, :] = lin_a * jax.nn.sigmoid(lin_b)

    n_win = rc + hist
    pr = lax.broadcasted_iota(jnp.int32, ((SUBLANES - 1) * n_win, n_win), 0)
    pc = lax.broadcasted_iota(jnp.int32, ((SUBLANES - 1) * n_win, n_win), 1)
    shift_mat = jnp.where(pr % n_win - pc == pr // n_win + 1, 1.0, 0.0).astype(BF16)
    for l0 in range(0, d, 2 * LANES):
        for r0 in range(0, ts, rc):
            win2 = abuf_ref[r0:r0 + n_win, l0:l0 + 2 * LANES]
            shifted2 = _dot(shift_mat, win2.astype(BF16))
            for half in range(2):
                lanes = slice(l0 + half * LANES, l0 + (half + 1) * LANES)
                cols = slice(half * LANES, (half + 1) * LANES)
                acc = jnp.broadcast_to(bdw_ref[:, lanes], (rc, LANES))
                for sft in range(SUBLANES):
                    moved = (win2[:, cols] if sft == 0
                             else shifted2[(sft - 1) * n_win:sft * n_win, cols])
                    for dly in range(sft, CONV_WIDTH, SUBLANES):
                        tap = CONV_WIDTH - 1 - dly
                        start = hist - (dly - sft)
                        acc = acc + wdw_ref[tap:tap + 1, lanes] * moved[start:start + rc]
                y_ref[r0:r0 + rc, lanes] = acc

    y = y_ref[...]
    mu = jnp.mean(y, axis=-1, keepdims=True)
    yc = y - mu
    var = jnp.mean(yc * yc, axis=-1, keepdims=True)
    yn = yc * lax.rsqrt(var + LN_EPS) * lng_ref[...] + lnb_ref[...]
    o_ref[0] = (yn * jax.nn.sigmoid(yn)).astype(BF16)


def _conv_front(h3, g, wg, bg, wdw, bdw, lng, lnb, *, ts=512, rc=64):
    b, s, d = h3.shape
    ts = min(ts, s)
    return pl.pallas_call(
        functools.partial(_conv_kernel, rc=rc),
        grid=(b, s // ts),
        in_specs=[pl.BlockSpec((1, ts, d), lambda bi, si: (bi, si, 0)),
                  _const_spec((1, d)), _const_spec((d, 2 * d)), _const_spec((1, 2 * d)),
                  _const_spec((CONV_WIDTH, d)), _const_spec((1, d)), _const_spec((1, d)),
                  _const_spec((1, d))],
        out_specs=pl.BlockSpec((1, ts, d), lambda bi, si: (bi, si, 0)),
        out_shape=jax.ShapeDtypeStruct((b, s, d), BF16),
        scratch_shapes=[pltpu.VMEM((ts + CONV_HIST, d), F32), pltpu.VMEM((ts, d), F32)],
        compiler_params=pltpu.CompilerParams(dimension_semantics=("arbitrary", "arbitrary"),
                                             vmem_limit_bytes=VMEM_LIMIT),
        name="conv_front",
    )(h3, g, wg, bg, wdw, bdw, lng, lnb)


def kernel(x, norm_mix_g, norm_ffn_g, w_in_ab, w_out_ab, hg_lb_logits, hg_norm_g, conv_w_glu,
           conv_b_glu, conv_w_dw, conv_b_dw, conv_ln_g, conv_ln_b, conv_w_pw, conv_b_pw,
           w_ff1, w_ff2, final_norm_g):
    b, s, d = x.shape
    depth = norm_mix_g.shape[0]
    row = lambda a: a.reshape(1, -1).astype(F32)
    zeros_d = jnp.zeros((1, d), F32)
    h = x.reshape(b * s, d)
    for layer in range(depth):
        j = layer // 2
        last = layer == depth - 1
        if layer % 2 == 0:
            q_scale = SB_HEAD_DIM ** -0.5 * LOG2E
            scale = jnp.where(jnp.arange(w_in_ab.shape[2]) < SB_WIDTH, q_scale, 1.0)
            w_in = (w_in_ab[j] * scale).astype(BF16)
            proj = _inproj(h, row(norm_mix_g[layer]), w_in).reshape(b, s, -1)
            o_sb = _sb_attention(proj).reshape(b * s, SB_WIDTH)
            o_hg = _hgrn2(proj, hg_lb_logits.astype(F32), hg_norm_g[j].astype(F32),
                          layer=layer).reshape(b * s, HG_WIDTH)
            mix = (o_sb, o_hg, 0, 0)
            wp, bp = w_out_ab[j].astype(BF16), zeros_d
        else:
            y = _conv_front(h.reshape(b, s, d), row(norm_mix_g[layer]),
                            conv_w_glu[j].astype(BF16), row(conv_b_glu[j]),
                            conv_w_dw[j].astype(F32), row(conv_b_dw[j]), row(conv_ln_g[j]),
                            row(conv_ln_b[j])).reshape(b * s, d)
            mix = (y, y, 0, 1)
            wp, bp = conv_w_pw[j].astype(BF16), row(conv_b_pw[j])
        h = _proj_mlp(*mix, h, wp, bp, row(norm_ffn_g[layer]), w_ff1[layer].astype(BF16),
                      w_ff2[layer].astype(BF16), row(final_norm_g), final_norm=last)
    return h.reshape(b, s, d)
```

```python
import functools
import math

import jax
import jax.numpy as jnp
from jax import lax
from jax.experimental import pallas as pl
from jax.experimental.pallas import tpu as pltpu

F32 = jnp.float32
BF16 = jnp.bfloat16

RMS_EPS = 1e-6
LN_EPS = 1e-5

SB_HEADS = 8
SB_HEAD_DIM = 64
SB_WIDTH = SB_HEADS * SB_HEAD_DIM
HG_HEADS = 4
HG_HEAD_DIM = 128
HG_WIDTH = HG_HEADS * HG_HEAD_DIM
CONV_WIDTH = 31

LANES = 128
SUBLANES = 8
LOG2E = math.log2(math.e)
SB_BLOCK = 128
SB_HALF = SB_BLOCK // 2
SB_UNDERFLOW_LOG2 = -92.0 * LOG2E
HG_STEP = 32
HG_SUB = 256
CONV_HIST = 32
VMEM_LIMIT = 56 * 1024 * 1024

_NT = (((1,), (1,)), ((), ()))
_TN = (((0,), (0,)), ((), ()))


def _rms(x, g):
    return x * lax.rsqrt(jnp.mean(x * x, axis=-1, keepdims=True) + RMS_EPS) * g


def _dot(a, b):
    return jnp.dot(a, b, preferred_element_type=F32)


def _const_spec(shape):
    nd = len(shape)
    return pl.BlockSpec(shape, lambda *_: (0,) * nd, pipeline_mode=pl.Buffered(1))


def _inproj_kernel(x_ref, g_ref, w_ref, o_ref, *, tn):
    u = _rms(x_ref[...], g_ref[...]).astype(BF16)
    for n in range(w_ref.shape[1] // tn):
        cols = slice(n * tn, (n + 1) * tn)
        o_ref[:, cols] = _dot(u, w_ref[:, cols]).astype(BF16)


def _inproj(x2d, g, w, *, tm=1024, tn=512):
    t, d = x2d.shape
    n = w.shape[1]
    tm = min(tm, t)
    return pl.pallas_call(
        functools.partial(_inproj_kernel, tn=tn),
        grid=(t // tm,),
        in_specs=[pl.BlockSpec((tm, d), lambda i: (i, 0)),
                  _const_spec((1, d)),
                  _const_spec((d, n))],
        out_specs=pl.BlockSpec((tm, n), lambda i: (i, 0)),
        out_shape=jax.ShapeDtypeStruct((t, n), BF16),
        compiler_params=pltpu.CompilerParams(dimension_semantics=("arbitrary",),
                                             vmem_limit_bytes=VMEM_LIMIT),
        name="inproj",
    )(x2d, g, w)


def _sb_kernel(q_ref, k_ref, v_ref, o_ref, r_ref, acc_ref):
    blk = SB_BLOCK
    n_sub = q_ref.shape[1] // blk
    qi0 = pl.program_id(1) * n_sub
    pairs = q_ref.shape[2] // LANES
    row = lax.broadcasted_iota(jnp.int32, (blk, blk), 0)
    col = lax.broadcasted_iota(jnp.int32, (blk, blk), 1)
    causal = col < row
    trow = lax.broadcasted_iota(jnp.int32, (blk, 2 * blk), 0)
    tcol = lax.broadcasted_iota(jnp.int32, (blk, 2 * blk), 1)
    tri_ones = jnp.where((trow > tcol) | (tcol >= blk), 1.0, 0.0).astype(BF16)

    def split(x):
        lo = lax.broadcasted_iota(jnp.int32, x.shape, 1) < SB_HEAD_DIM
        zero = jnp.zeros_like(x)
        return (jnp.where(lo, x, zero), jnp.where(lo, zero, x))

    def sweep(groups, fresh):
        units = [(p, groups) for p in range(pairs)]
        z_of, mid_of, vs_of = {}, {}, {}

        def stage_scores(u):
            p, grps = units[u]
            lanes = slice(p * LANES, (p + 1) * LANES)
            vs_of[u], z_of[u] = [], []
            for q_rows, windows, _ in grps:
                q2 = jnp.concatenate(split(q_ref[0, q_rows, lanes]), axis=0)
                vs_of[u].append([v_ref[0, w, lanes] for w in windows])
                z_of[u].append([lax.dot_general(q2, k_ref[0, w, lanes], _NT,
                                                preferred_element_type=F32) for w in windows])

        def stage_sums(u):
            _, grps = units[u]
            log_betas, log_keeps = [], []
            for gi, (_, windows, mask) in enumerate(grps):
                mask2 = None if mask is None else jnp.concatenate([mask, mask], axis=0)
                for j in range(len(windows)):
                    z = z_of[u][gi][j]
                    log_beta = jnp.minimum(z, 0.0) - jnp.log2(1.0 + jnp.exp2(-jnp.abs(z)))
                    log_keep = log_beta - z
                    if mask2 is not None and j == 0:
                        log_keep = jnp.where(mask2, log_keep, 0.0)
                    log_betas.append(log_beta)
                    log_keeps.append(log_keep.astype(BF16))
            sums = _dot(jnp.concatenate(log_keeps, axis=0), tri_ones)
            mid_of[u], at = [], 0
            for log_beta in log_betas:
                n = log_beta.shape[0]
                mid_of[u].append((log_beta, sums[at:at + n, :blk], sums[at:at + n, blk:]))
                at += n
            del z_of[u]

        def stage_values(u):
            p, grps = units[u]
            lanes = slice(p * LANES, (p + 1) * LANES)
            mids = iter(mid_of[u])
            for gi, (q_rows, windows, mask) in enumerate(grps):
                n = q_rows.stop - q_rows.start
                mask2 = None if mask is None else jnp.concatenate([mask, mask], axis=0)
                r = None if fresh else jnp.concatenate(
                    [r_ref[2 * p, q_rows, :], r_ref[2 * p + 1, q_rows, :]], axis=0)
                pv2 = None
                for j in range(len(windows)):
                    log_beta, later, total = next(mids)
                    e = log_beta + later
                    if r is not None:
                        e = e + r
                    w = jnp.exp2(e)
                    if mask2 is not None and j == 0:
                        w = jnp.where(mask2, w, 0.0)
                    r = total if r is None else r + total
                    part = _dot(w.astype(BF16), vs_of[u][gi][j])
                    pv2 = part if pv2 is None else pv2 + part
                r_ref[2 * p, q_rows, :] = r[:n]
                r_ref[2 * p + 1, q_rows, :] = r[n:]
                lo = lax.broadcasted_iota(jnp.int32, (n, LANES), 1) < SB_HEAD_DIM
                pv = jnp.where(lo, pv2[:n], pv2[n:])
                if fresh:
                    acc_ref[q_rows, lanes] = pv
                else:
                    acc_ref[q_rows, lanes] += pv
            del mid_of[u], vs_of[u]

        for t in range(len(units) + 2):
            if t < len(units):
                stage_scores(t)
            if 0 <= t - 1 < len(units):
                stage_sums(t - 1)
            if 0 <= t - 2 < len(units):
                stage_values(t - 2)

    def key_block(kb):
        return pl.ds(pl.multiple_of(kb * blk, blk), blk)

    def window_sweep():
        win = 2 * blk
        wr = lax.broadcasted_iota(jnp.int32, (win, win), 0)
        wc = lax.broadcasted_iota(jnp.int32, (win, win), 1)
        tri = jnp.where(wr > wc, 1.0, 0.0).astype(BF16)
        hr = lax.broadcasted_iota(jnp.int32, (2 * SB_HALF, blk), 0) % SB_HALF
        hc = lax.broadcasted_iota(jnp.int32, (2 * SB_HALF, blk), 1)
        newest_ok = hc < hr + SB_HALF
        halves = [(slice(first, first + SB_HALF),
                   pl.ds(pl.multiple_of(qi0 * blk + first + SB_HALF - win, SB_HALF), win))
                  for first in range(0, n_sub * blk, SB_HALF)]
        z_of, mid_of, v_of = {}, {}, {}
        worst = [jnp.full((1, 1), -jnp.inf, F32)]

        def masked(x):
            return jnp.concatenate([x[:, :blk], jnp.where(newest_ok, x[:, blk:], 0.0)], axis=1)

        def stage_scores(p):
            lanes = slice(p * LANES, (p + 1) * LANES)
            z_of[p], v_of[p] = [], []
            for q_rows, keys in halves:
                q2 = jnp.concatenate(split(q_ref[0, q_rows, lanes]), axis=0)
                v_of[p].append(v_ref[0, keys, lanes])
                z_of[p].append(lax.dot_general(q2, k_ref[0, keys, lanes], _NT,
                                               preferred_element_type=F32))

        def stage_sums(p):
            log_betas, log_keeps = [], []
            for z in z_of[p]:
                log_beta = jnp.minimum(z, 0.0) - jnp.log2(1.0 + jnp.exp2(-jnp.abs(z)))
                log_keep = masked(log_beta - z)
                log_betas.append(log_beta)
                log_keeps.append(log_keep)
            stacked = jnp.concatenate(log_keeps, axis=0)
            later = _dot(stacked.astype(BF16), tri)
            totals = later[:, 0:1] + stacked[:, 0:1]
            worst[0] = jnp.maximum(worst[0], jnp.max(totals, axis=0, keepdims=True))
            n = 2 * SB_HALF
            mid_of[p] = [(lb, later[i * n:(i + 1) * n]) for i, lb in enumerate(log_betas)]
            del z_of[p]

        def stage_values(p):
            lanes = slice(p * LANES, (p + 1) * LANES)
            lo = lax.broadcasted_iota(jnp.int32, (SB_HALF, LANES), 1) < SB_HEAD_DIM
            for (q_rows, _), (log_beta, later), v in zip(halves, mid_of[p], v_of[p]):
                w = masked(jnp.exp2(log_beta + later))
                pv2 = _dot(w.astype(BF16), v)
                acc_ref[q_rows, lanes] = jnp.where(lo, pv2[:SB_HALF], pv2[SB_HALF:])
            del mid_of[p], v_of[p]

        for t in range(pairs + 2):
            if t < pairs:
                stage_scores(t)
            if 0 <= t - 1 < pairs:
                stage_sums(t - 1)
            if 0 <= t - 2 < pairs:
                stage_values(t - 2)
        r_ref[0, 0:SUBLANES, :] = jnp.broadcast_to(worst[0], (SUBLANES, blk))

    @pl.when(qi0 < 2)
    def _():
        r_ref[0, 0:SUBLANES, :] = jnp.zeros((SUBLANES, blk), F32)

    @pl.when(qi0 >= 2)
    def _():
        window_sweep()

    redo = jnp.max(r_ref[0, 0:SUBLANES, :]) > SB_UNDERFLOW_LOG2

    @pl.when(redo)
    def _():
        for sub in range(n_sub):
            q_rows = slice(sub * blk, (sub + 1) * blk)
            sweep([(q_rows, [key_block(qi0 + sub)], causal)], True)

            def cond(c):
                kb, m = c
                return jnp.logical_and(kb >= 0, m > SB_UNDERFLOW_LOG2)

            def body(c, q_rows=q_rows):
                kb, _ = c
                sweep([(q_rows, [key_block(kb)], None)], False)
                return kb - 1, jnp.max(r_ref[:, q_rows, :])

            lax.while_loop(cond, body, (qi0 + sub - 1, jnp.max(r_ref[:, q_rows, :])))

    o_ref[0] = acc_ref[...].astype(BF16)


def _sb_attention(proj3, *, q_rows=4 * SB_BLOCK):
    b, s, _ = proj3.shape
    q_rows = min(q_rows, s)
    kv_spec = lambda c: pl.BlockSpec((1, s, SB_WIDTH), lambda bi, qi: (bi, 0, c),
                                     pipeline_mode=pl.Buffered(1))
    return pl.pallas_call(
        _sb_kernel,
        grid=(b, s // q_rows),
        in_specs=[pl.BlockSpec((1, q_rows, SB_WIDTH), lambda bi, qi: (bi, qi, 0)),
                  kv_spec(1), kv_spec(2)],
        out_specs=pl.BlockSpec((1, q_rows, SB_WIDTH), lambda bi, qi: (bi, qi, 0)),
        out_shape=jax.ShapeDtypeStruct((b, s, SB_WIDTH), BF16),
        scratch_shapes=[pltpu.VMEM((SB_HEADS, q_rows, SB_BLOCK), F32),
                        pltpu.VMEM((q_rows, SB_WIDTH), F32)],
        compiler_params=pltpu.CompilerParams(dimension_semantics=("arbitrary", "arbitrary"),
                                             vmem_limit_bytes=VMEM_LIMIT),
        name="sb_attention",
    )(proj3, proj3, proj3)


def _hg_kernel(q_ref, f_ref, i_ref, gate_ref, lbl_ref, ng_ref, o_ref, state_ref, *, layer):
    si = pl.program_id(2)
    step, sub = HG_STEP, HG_SUB
    half = step // 2

    @pl.when(si == 0)
    def _():
        state_ref[...] = jnp.zeros_like(state_ref)

    logits = lbl_ref[...]
    e = jnp.exp(logits - jnp.max(logits, axis=0, keepdims=True))
    lb = jnp.sum(e[:layer + 1], axis=0, keepdims=True) / jnp.sum(e, axis=0, keepdims=True)
    ng = ng_ref[0]

    r = lax.broadcasted_iota(jnp.int32, (sub, sub), 0)
    c = lax.broadcasted_iota(jnp.int32, (sub, sub), 1)
    same = (r // step) == (c // step)
    incl = jnp.where(same & (c <= r), 1.0, 0.0)
    to_mid = jnp.where(same & (c % step < half), 1.0, 0.0)
    m_q = (incl - to_mid).astype(BF16)
    m_k = jnp.where(same & (c > r), 1.0, 0.0).astype(BF16)
    m_g = incl.astype(BF16)
    tr = lax.broadcasted_iota(jnp.int32, (step, step), 0)
    tc = lax.broadcasted_iota(jnp.int32, (step, step), 1)
    tril = tr >= tc
    per_grp = LANES // step
    lane_step = lax.broadcasted_iota(jnp.int32, (HG_HEAD_DIM, LANES), 1) // step

    def body(j, carry):
        rows = pl.ds(pl.multiple_of(j * sub, sub), sub)
        q = q_ref[0, rows, :].astype(F32)
        f = lb + (1.0 - lb) * jax.nn.sigmoid(f_ref[0, rows, :].astype(F32))
        v = i_ref[0, rows, :]
        g = jnp.log(f)
        kk = 1.0 - f
        g_hi = g.astype(BF16)
        g_lo = (g - g_hi.astype(F32)).astype(BF16)
        d_q = _dot(m_q, g_hi) + _dot(m_q, g_lo)
        d_k = _dot(m_k, g_hi) + _dot(m_k, g_lo)
        e_g = jnp.exp(_dot(m_g, g_hi) + _dot(m_g, g_lo))
        q_intra = (q * jnp.exp(d_q)).astype(BF16)
        k_intra = (kk * jnp.exp(-d_q)).astype(BF16)
        k_state = (kk * jnp.exp(d_k)).astype(BF16)
        q_inter = (q * e_g).astype(BF16)
        steps = [slice(n * step, (n + 1) * step) for n in range(sub // step)]
        scores = [lax.dot_general(q_intra[sl], k_intra[sl], _NT, preferred_element_type=F32)
                  for sl in steps]
        upd_t = [lax.dot_general(v[sl], k_state[sl], _TN, preferred_element_type=F32)
                 for sl in steps]
        intra = [_dot(jnp.where(tril, sc, 0.0).astype(BF16), v[sl])
                 for sc, sl in zip(scores, steps)]
        state_t = state_ref[...]
        inter = []
        for gi in range(sub // LANES):
            q_grp = q_inter[gi * LANES:(gi + 1) * LANES]
            o_t = None
            for m in range(per_grp):
                n = gi * per_grp + m
                res = lax.dot_general(state_t.astype(BF16), q_grp, _NT, preferred_element_type=F32)
                o_t = res if o_t is None else jnp.where(lane_step == m, res, o_t)
                state_t = state_t * e_g[(n + 1) * step - 1:(n + 1) * step] + upd_t[n]
            inter.append(o_t.T)
        state_ref[...] = state_t
        o = jnp.concatenate(intra, axis=0) + jnp.concatenate(inter, axis=0)
        o = o * lax.rsqrt(jnp.mean(o * o, axis=-1, keepdims=True) + RMS_EPS) * ng
        o = o * jax.nn.silu(gate_ref[0, rows, :].astype(F32))
        o_ref[0, rows, :] = o.astype(BF16)
        return carry

    lax.fori_loop(0, q_ref.shape[1] // sub, body, 0, unroll=True)


def _hgrn2(proj3, lb_logits, norm_g, *, layer, ts=4096):
    b, s, _ = proj3.shape
    ts = min(ts, s)
    base = 3 * SB_WIDTH // LANES
    depth1 = lb_logits.shape[0]

    def col(k):
        return pl.BlockSpec((1, ts, LANES), lambda bi, h, si: (bi, si, base + k * HG_HEADS + h))

    return pl.pallas_call(
        functools.partial(_hg_kernel, layer=layer),
        grid=(b, HG_HEADS, s // ts),
        in_specs=[col(0), col(1), col(2), col(3),
                  pl.BlockSpec((depth1, LANES), lambda bi, h, si: (0, h)),
                  pl.BlockSpec((1, 1, LANES), lambda bi, h, si: (h, 0, 0))],
        out_specs=pl.BlockSpec((1, ts, LANES), lambda bi, h, si: (bi, si, h)),
        out_shape=jax.ShapeDtypeStruct((b, s, HG_WIDTH), BF16),
        scratch_shapes=[pltpu.VMEM((HG_HEAD_DIM, HG_HEAD_DIM), F32)],
        compiler_params=pltpu.CompilerParams(
            dimension_semantics=("arbitrary", "arbitrary", "arbitrary"),
            vmem_limit_bytes=VMEM_LIMIT),
        name="hgrn2",
    )(proj3, proj3, proj3, proj3, lb_logits, norm_g.reshape(HG_HEADS, 1, HG_HEAD_DIM))


def _proj_mlp_kernel(ma_ref, mb_ref, h_ref, wp_ref, bp_ref, g_ref, w1_ref, w2_ref, gf_ref,
                     o_ref, *, tf, final_norm):
    half = ma_ref.shape[1]
    h1 = (h_ref[...] + _dot(ma_ref[...], wp_ref[:half, :]) + _dot(mb_ref[...], wp_ref[half:, :])
          + bp_ref[...])
    u = _rms(h1, g_ref[...]).astype(BF16)
    acc = h1
    for c in range(w1_ref.shape[1] // tf):
        a = jnp.maximum(_dot(u, w1_ref[:, c * tf:(c + 1) * tf]), 0.0)
        acc = acc + _dot((a * a).astype(BF16), w2_ref[c * tf:(c + 1) * tf, :])
    if final_norm:
        acc = _rms(acc, gf_ref[...])
    o_ref[...] = acc


def _proj_mlp(ma, mb, cols_a, cols_b, h2d, wp, bp, g, w1, w2, gf, *, final_norm, tm=1024, tf=1024):
    t, d = h2d.shape
    half = d // 2
    dff = w1.shape[1]
    tm = min(tm, t)
    return pl.pallas_call(
        functools.partial(_proj_mlp_kernel, tf=tf, final_norm=final_norm),
        grid=(t // tm,),
        in_specs=[pl.BlockSpec((tm, half), lambda i: (i, cols_a)),
                  pl.BlockSpec((tm, half), lambda i: (i, cols_b)),
                  pl.BlockSpec((tm, d), lambda i: (i, 0)),
                  _const_spec((d, d)), _const_spec((1, d)), _const_spec((1, d)),
                  _const_spec((d, dff)), _const_spec((dff, d)), _const_spec((1, d))],
        out_specs=pl.BlockSpec((tm, d), lambda i: (i, 0)),
        out_shape=jax.ShapeDtypeStruct((t, d), F32),
        compiler_params=pltpu.CompilerParams(dimension_semantics=("arbitrary",),
                                             vmem_limit_bytes=VMEM_LIMIT),
        name="proj_mlp_final" if final_norm else "proj_mlp",
    )(ma, mb, h2d, wp, bp, g, w1, w2, gf)


def _conv_kernel(h_ref, g_ref, wg_ref, bg_ref, wdw_ref, bdw_ref, lng_ref, lnb_ref, o_ref,
                 abuf_ref, y_ref, *, rc):
    si = pl.program_id(1)
    ts, d = h_ref.shape[1], h_ref.shape[2]
    hist = CONV_HIST

    @pl.when(si == 0)
    def _():
        abuf_ref[0:hist, :] = jnp.zeros((hist, d), F32)

    @pl.when(si > 0)
    def _():
        abuf_ref[0:hist, :] = abuf_ref[ts:ts + hist, :]

    u = _rms(h_ref[0], g_ref[...]).astype(BF16)
    lin_a = _dot(u, wg_ref[:, :d]) + bg_ref[:, :d]
    lin_b = _dot(u, wg_ref[:, d:]) + bg_ref[:, d:]
    abuf_ref[hist:, :] = lin_a * jax.nn.sigmoid(lin_b)

    n_win = rc + hist
    pr = lax.broadcasted_iota(jnp.int32, ((SUBLANES - 1) * n_win, n_win), 0)
    pc = lax.broadcasted_iota(jnp.int32, ((SUBLANES - 1) * n_win, n_win), 1)
    shift_mat = jnp.where(pr % n_win - pc == pr // n_win + 1, 1.0, 0.0).astype(BF16)
    for l0 in range(0, d, 2 * LANES):
        for r0 in range(0, ts, rc):
            win2 = abuf_ref[r0:r0 + n_win, l0:l0 + 2 * LANES]
            shifted2 = _dot(shift_mat, win2.astype(BF16))
            for half in range(2):
                lanes = slice(l0 + half * LANES, l0 + (half + 1) * LANES)
                cols = slice(half * LANES, (half + 1) * LANES)
                acc = jnp.broadcast_to(bdw_ref[:, lanes], (rc, LANES))
                for sft in range(SUBLANES):
                    moved = (win2[:, cols] if sft == 0
                             else shifted2[(sft - 1) * n_win:sft * n_win, cols])
                    for dly in range(sft, CONV_WIDTH, SUBLANES):
                        tap = CONV_WIDTH - 1 - dly
                        start = hist - (dly - sft)
                        acc = acc + wdw_ref[tap:tap + 1, lanes] * moved[start:start + rc]
                y_ref[r0:r0 + rc, lanes] = acc

    y = y_ref[...]
    mu = jnp.mean(y, axis=-1, keepdims=True)
    yc = y - mu
    var = jnp.mean(yc * yc, axis=-1, keepdims=True)
    yn = yc * lax.rsqrt(var + LN_EPS) * lng_ref[...] + lnb_ref[...]
    o_ref[0] = (yn * jax.nn.sigmoid(yn)).astype(BF16)


def _conv_front(h3, g, wg, bg, wdw, bdw, lng, lnb, *, ts=1024, rc=64):
    b, s, d = h3.shape
    ts = min(ts, s)
    return pl.pallas_call(
        functools.partial(_conv_kernel, rc=rc),
        grid=(b, s // ts),
        in_specs=[pl.BlockSpec((1, ts, d), lambda bi, si: (bi, si, 0)),
                  _const_spec((1, d)), _const_spec((d, 2 * d)), _const_spec((1, 2 * d)),
                  _const_spec((CONV_WIDTH, d)), _const_spec((1, d)), _const_spec((1, d)),
                  _const_spec((1, d))],
        out_specs=pl.BlockSpec((1, ts, d), lambda bi, si: (bi, si, 0)),
        out_shape=jax.ShapeDtypeStruct((b, s, d), BF16),
        scratch_shapes=[pltpu.VMEM((ts + CONV_HIST, d), F32), pltpu.VMEM((ts, d), F32)],
        compiler_params=pltpu.CompilerParams(dimension_semantics=("arbitrary", "arbitrary"),
                                             vmem_limit_bytes=VMEM_LIMIT),
        name="conv_front",
    )(h3, g, wg, bg, wdw, bdw, lng, lnb)


def kernel(x, norm_mix_g, norm_ffn_g, w_in_ab, w_out_ab, hg_lb_logits, hg_norm_g, conv_w_glu,
           conv_b_glu, conv_w_dw, conv_b_dw, conv_ln_g, conv_ln_b, conv_w_pw, conv_b_pw,
           w_ff1, w_ff2, final_norm_g):
    b, s, d = x.shape
    depth = norm_mix_g.shape[0]
    row = lambda a: a.reshape(1, -1).astype(F32)
    zeros_d = jnp.zeros((1, d), F32)
    h = x.reshape(b * s, d)
    for layer in range(depth):
        j = layer // 2
        last = layer == depth - 1
        if layer % 2 == 0:
            q_scale = SB_HEAD_DIM ** -0.5 * LOG2E
            scale = jnp.where(jnp.arange(w_in_ab.shape[2]) < SB_WIDTH, q_scale, 1.0)
            w_in = (w_in_ab[j] * scale).astype(BF16)
            proj = _inproj(h, row(norm_mix_g[layer]), w_in).reshape(b, s, -1)
            o_sb = _sb_attention(proj).reshape(b * s, SB_WIDTH)
            o_hg = _hgrn2(proj, hg_lb_logits.astype(F32), hg_norm_g[j].astype(F32),
                          layer=layer).reshape(b * s, HG_WIDTH)
            mix = (o_sb, o_hg, 0, 0)
            wp, bp = w_out_ab[j].astype(BF16), zeros_d
        else:
            y = _conv_front(h.reshape(b, s, d), row(norm_mix_g[layer]),
                            conv_w_glu[j].astype(BF16), row(conv_b_glu[j]),
                            conv_w_dw[j].astype(F32), row(conv_b_dw[j]), row(conv_ln_g[j]),
                            row(conv_ln_b[j])).reshape(b * s, d)
            mix = (y, y, 0, 1)
            wp, bp = conv_w_pw[j].astype(BF16), row(conv_b_pw[j])
        h = _proj_mlp(*mix, h, wp, bp, row(norm_ffn_g[layer]), w_ff1[layer].astype(BF16),
                      w_ff2[layer].astype(BF16), row(final_norm_g), final_norm=last)
    return h.reshape(b, s, d)
```

```python
import functools
import math

import jax
import jax.numpy as jnp
from jax import lax
from jax.experimental import pallas as pl
from jax.experimental.pallas import tpu as pltpu

F32 = jnp.float32
BF16 = jnp.bfloat16

RMS_EPS = 1e-6
LN_EPS = 1e-5

SB_HEADS = 8
SB_HEAD_DIM = 64
SB_WIDTH = SB_HEADS * SB_HEAD_DIM
HG_HEADS = 4
HG_HEAD_DIM = 128
HG_WIDTH = HG_HEADS * HG_HEAD_DIM
CONV_WIDTH = 31

LANES = 128
SUBLANES = 8
LOG2E = math.log2(math.e)
SB_BLOCK = 128
SB_HALF = SB_BLOCK // 2
SB_UNDERFLOW_LOG2 = -92.0 * LOG2E
HG_STEP = 32
HG_SUB = 256
CONV_HIST = 32
VMEM_LIMIT = 56 * 1024 * 1024

_NT = (((1,), (1,)), ((), ()))
_TN = (((0,), (0,)), ((), ()))


def _rms(x, g):
    return x * lax.rsqrt(jnp.mean(x * x, axis=-1, keepdims=True) + RMS_EPS) * g


def _dot(a, b):
    return jnp.dot(a, b, preferred_element_type=F32)


def _const_spec(shape):
    nd = len(shape)
    return pl.BlockSpec(shape, lambda *_: (0,) * nd, pipeline_mode=pl.Buffered(1))


def _inproj_kernel(x_ref, g_ref, w_ref, o_ref, *, tn):
    u = _rms(x_ref[...], g_ref[...]).astype(BF16)
    for n in range(w_ref.shape[1] // tn):
        cols = slice(n * tn, (n + 1) * tn)
        o_ref[:, cols] = _dot(u, w_ref[:, cols]).astype(BF16)


def _inproj(x2d, g, w, *, tm=1024, tn=512):
    t, d = x2d.shape
    n = w.shape[1]
    tm = min(tm, t)
    return pl.pallas_call(
        functools.partial(_inproj_kernel, tn=tn),
        grid=(t // tm,),
        in_specs=[pl.BlockSpec((tm, d), lambda i: (i, 0)),
                  _const_spec((1, d)),
                  _const_spec((d, n))],
        out_specs=pl.BlockSpec((tm, n), lambda i: (i, 0)),
        out_shape=jax.ShapeDtypeStruct((t, n), BF16),
        compiler_params=pltpu.CompilerParams(dimension_semantics=("arbitrary",),
                                             vmem_limit_bytes=VMEM_LIMIT),
        name="inproj",
    )(x2d, g, w)


def _sb_kernel(q_ref, k_ref, v_ref, o_ref, r_ref, acc_ref):
    blk = SB_BLOCK
    n_sub = q_ref.shape[1] // blk
    qi0 = pl.program_id(1) * n_sub
    pairs = q_ref.shape[2] // LANES
    row = lax.broadcasted_iota(jnp.int32, (blk, blk), 0)
    col = lax.broadcasted_iota(jnp.int32, (blk, blk), 1)
    causal = col < row
    trow = lax.broadcasted_iota(jnp.int32, (blk, 2 * blk), 0)
    tcol = lax.broadcasted_iota(jnp.int32, (blk, 2 * blk), 1)
    tri_ones = jnp.where((trow > tcol) | (tcol >= blk), 1.0, 0.0).astype(BF16)

    def split(x):
        lo = lax.broadcasted_iota(jnp.int32, x.shape, 1) < SB_HEAD_DIM
        zero = jnp.zeros_like(x)
        return (jnp.where(lo, x, zero), jnp.where(lo, zero, x))

    def sweep(groups, fresh):
        units = [(p, groups) for p in range(pairs)]
        z_of, mid_of, vs_of = {}, {}, {}

        def stage_scores(u):
            p, grps = units[u]
            lanes = slice(p * LANES, (p + 1) * LANES)
            vs_of[u], z_of[u] = [], []
            for q_rows, windows, _ in grps:
                q2 = jnp.concatenate(split(q_ref[0, q_rows, lanes]), axis=0)
                vs_of[u].append([v_ref[0, w, lanes] for w in windows])
                z_of[u].append([lax.dot_general(q2, k_ref[0, w, lanes], _NT,
                                                preferred_element_type=F32) for w in windows])

        def stage_sums(u):
            _, grps = units[u]
            log_betas, log_keeps = [], []
            for gi, (_, windows, mask) in enumerate(grps):
                mask2 = None if mask is None else jnp.concatenate([mask, mask], axis=0)
                for j in range(len(windows)):
                    z = z_of[u][gi][j]
                    log_beta = jnp.minimum(z, 0.0) - jnp.log2(1.0 + jnp.exp2(-jnp.abs(z)))
                    log_keep = log_beta - z
                    if mask2 is not None and j == 0:
                        log_keep = jnp.where(mask2, log_keep, 0.0)
                    log_betas.append(log_beta)
                    log_keeps.append(log_keep.astype(BF16))
            sums = _dot(jnp.concatenate(log_keeps, axis=0), tri_ones)
            mid_of[u], at = [], 0
            for log_beta in log_betas:
                n = log_beta.shape[0]
                mid_of[u].append((log_beta, sums[at:at + n, :blk], sums[at:at + n, blk:]))
                at += n
            del z_of[u]

        def stage_values(u):
            p, grps = units[u]
            lanes = slice(p * LANES, (p + 1) * LANES)
            mids = iter(mid_of[u])
            for gi, (q_rows, windows, mask) in enumerate(grps):
                n = q_rows.stop - q_rows.start
                mask2 = None if mask is None else jnp.concatenate([mask, mask], axis=0)
                r = None if fresh else jnp.concatenate(
                    [r_ref[2 * p, q_rows, :], r_ref[2 * p + 1, q_rows, :]], axis=0)
                pv2 = None
                for j in range(len(windows)):
                    log_beta, later, total = next(mids)
                    e = log_beta + later
                    if r is not None:
                        e = e + r
                    w = jnp.exp2(e)
                    if mask2 is not None and j == 0:
                        w = jnp.where(mask2, w, 0.0)
                    r = total if r is None else r + total
                    part = _dot(w.astype(BF16), vs_of[u][gi][j])
                    pv2 = part if pv2 is None else pv2 + part
                r_ref[2 * p, q_rows, :] = r[:n]
                r_ref[2 * p + 1, q_rows, :] = r[n:]
                lo = lax.broadcasted_iota(jnp.int32, (n, LANES), 1) < SB_HEAD_DIM
                pv = jnp.where(lo, pv2[:n], pv2[n:])
                if fresh:
                    acc_ref[q_rows, lanes] = pv
                else:
                    acc_ref[q_rows, lanes] += pv
            del mid_of[u], vs_of[u]

        for t in range(len(units) + 2):
            if t < len(units):
                stage_scores(t)
            if 0 <= t - 1 < len(units):
                stage_sums(t - 1)
            if 0 <= t - 2 < len(units):
                stage_values(t - 2)

    def key_block(kb):
        return pl.ds(pl.multiple_of(kb * blk, blk), blk)

    def window_sweep(first_sub):
        win = 2 * blk
        wr = lax.broadcasted_iota(jnp.int32, (win, win), 0)
        wc = lax.broadcasted_iota(jnp.int32, (win, win), 1)
        tri = jnp.where(wr > wc, 1.0, 0.0).astype(BF16)
        hr = lax.broadcasted_iota(jnp.int32, (2 * SB_HALF, blk), 0) % SB_HALF
        hc = lax.broadcasted_iota(jnp.int32, (2 * SB_HALF, blk), 1)
        newest_ok = hc < hr + SB_HALF
        halves = [(slice(first, first + SB_HALF),
                   pl.ds(pl.multiple_of(qi0 * blk + first + SB_HALF - win, SB_HALF), win))
                  for first in range(first_sub * blk, n_sub * blk, SB_HALF)]
        z_of, mid_of, v_of = {}, {}, {}
        worst = [jnp.full((1, 1), -jnp.inf, F32)]

        def masked(x):
            return jnp.concatenate([x[:, :blk], jnp.where(newest_ok, x[:, blk:], 0.0)], axis=1)

        def stage_scores(p):
            lanes = slice(p * LANES, (p + 1) * LANES)
            z_of[p], v_of[p] = [], []
            for q_rows, keys in halves:
                q2 = jnp.concatenate(split(q_ref[0, q_rows, lanes]), axis=0)
                v_of[p].append(v_ref[0, keys, lanes])
                z_of[p].append(lax.dot_general(q2, k_ref[0, keys, lanes], _NT,
                                               preferred_element_type=F32))

        def stage_sums(p):
            log_betas, log_keeps = [], []
            for z in z_of[p]:
                log_beta = jnp.minimum(z, 0.0) - jnp.log2(1.0 + jnp.exp2(-jnp.abs(z)))
                log_keep = masked(log_beta - z)
                log_betas.append(log_beta)
                log_keeps.append(log_keep)
            stacked = jnp.concatenate(log_keeps, axis=0)
            later = _dot(stacked.astype(BF16), tri)
            totals = later[:, 0:1] + stacked[:, 0:1]
            worst[0] = jnp.maximum(worst[0], jnp.max(totals, axis=0, keepdims=True))
            n = 2 * SB_HALF
            mid_of[p] = [(lb, later[i * n:(i + 1) * n]) for i, lb in enumerate(log_betas)]
            del z_of[p]

        def stage_values(p):
            lanes = slice(p * LANES, (p + 1) * LANES)
            lo = lax.broadcasted_iota(jnp.int32, (SB_HALF, LANES), 1) < SB_HEAD_DIM
            for (q_rows, _), (log_beta, later), v in zip(halves, mid_of[p], v_of[p]):
                w = masked(jnp.exp2(log_beta + later))
                pv2 = _dot(w.astype(BF16), v)
                acc_ref[q_rows, lanes] = jnp.where(lo, pv2[:SB_HALF], pv2[SB_HALF:])
            del mid_of[p], v_of[p]

        for t in range(pairs + 2):
            if t < pairs:
                stage_scores(t)
            if 0 <= t - 1 < pairs:
                stage_sums(t - 1)
            if 0 <= t - 2 < pairs:
                stage_values(t - 2)
        r_ref[0, 0:SUBLANES, :] = jnp.broadcast_to(worst[0], (SUBLANES, blk))

    seq_start = qi0 < 2

    @pl.when(seq_start)
    def _():
        if n_sub > 2:
            window_sweep(2)
        else:
            r_ref[0, 0:SUBLANES, :] = jnp.full((SUBLANES, blk), -jnp.inf, F32)

    @pl.when(jnp.logical_not(seq_start))
    def _():
        window_sweep(0)

    redo = jnp.max(r_ref[0, 0:SUBLANES, :]) > SB_UNDERFLOW_LOG2

    for sub in range(n_sub):
        q_rows = slice(sub * blk, (sub + 1) * blk)

        @pl.when(jnp.logical_or(redo, seq_start) if sub < 2 else redo)
        def _():
            sweep([(q_rows, [key_block(qi0 + sub)], causal)], True)

            def cond(c):
                kb, m = c
                return jnp.logical_and(kb >= 0, m > SB_UNDERFLOW_LOG2)

            def body(c):
                kb, _ = c
                sweep([(q_rows, [key_block(kb)], None)], False)
                return kb - 1, jnp.max(r_ref[:, q_rows, :])

            lax.while_loop(cond, body, (qi0 + sub - 1, jnp.max(r_ref[:, q_rows, :])))

    o_ref[0] = acc_ref[...].astype(BF16)


def _sb_attention(proj3, *, q_rows=4 * SB_BLOCK):
    b, s, _ = proj3.shape
    q_rows = min(q_rows, s)
    kv_spec = lambda c: pl.BlockSpec((1, s, SB_WIDTH), lambda bi, qi: (bi, 0, c))
    return pl.pallas_call(
        _sb_kernel,
        grid=(b, s // q_rows),
        in_specs=[pl.BlockSpec((1, q_rows, SB_WIDTH), lambda bi, qi: (bi, qi, 0)),
                  kv_spec(1), kv_spec(2)],
        out_specs=pl.BlockSpec((1, q_rows, SB_WIDTH), lambda bi, qi: (bi, qi, 0)),
        out_shape=jax.ShapeDtypeStruct((b, s, SB_WIDTH), BF16),
        scratch_shapes=[pltpu.VMEM((SB_HEADS, q_rows, SB_BLOCK), F32),
                        pltpu.VMEM((q_rows, SB_WIDTH), F32)],
        compiler_params=pltpu.CompilerParams(dimension_semantics=("arbitrary", "arbitrary"),
                                             vmem_limit_bytes=VMEM_LIMIT),
        name="sb_attention",
    )(proj3, proj3, proj3)


def _hg_kernel(q_ref, f_ref, i_ref, gate_ref, lbl_ref, ng_ref, o_ref, state_ref, *, layer):
    si = pl.program_id(2)
    step, sub = HG_STEP, HG_SUB
    half = step // 2

    @pl.when(si == 0)
    def _():
        state_ref[...] = jnp.zeros_like(state_ref)

    logits = lbl_ref[...]
    e = jnp.exp(logits - jnp.max(logits, axis=0, keepdims=True))
    lb = jnp.sum(e[:layer + 1], axis=0, keepdims=True) / jnp.sum(e, axis=0, keepdims=True)
    ng = ng_ref[0]

    r = lax.broadcasted_iota(jnp.int32, (sub, sub), 0)
    c = lax.broadcasted_iota(jnp.int32, (sub, sub), 1)
    same = (r // step) == (c // step)
    incl = jnp.where(same & (c <= r), 1.0, 0.0)
    to_mid = jnp.where(same & (c % step < half), 1.0, 0.0)
    m_q = (incl - to_mid).astype(BF16)
    m_k = jnp.where(same & (c > r), 1.0, 0.0).astype(BF16)
    m_g = incl.astype(BF16)
    tr = lax.broadcasted_iota(jnp.int32, (step, step), 0)
    tc = lax.broadcasted_iota(jnp.int32, (step, step), 1)
    tril = tr >= tc
    per_grp = LANES // step
    lane_step = lax.broadcasted_iota(jnp.int32, (HG_HEAD_DIM, LANES), 1) // step

    def body(j, carry):
        rows = pl.ds(pl.multiple_of(j * sub, sub), sub)
        q = q_ref[0, rows, :].astype(F32)
        f = lb + (1.0 - lb) * jax.nn.sigmoid(f_ref[0, rows, :].astype(F32))
        v = i_ref[0, rows, :]
        g = jnp.log(f)
        kk = 1.0 - f
        g_hi = g.astype(BF16)
        g_lo = (g - g_hi.astype(F32)).astype(BF16)
        d_q = _dot(m_q, g_hi) + _dot(m_q, g_lo)
        d_k = _dot(m_k, g_hi) + _dot(m_k, g_lo)
        e_g = jnp.exp(_dot(m_g, g_hi) + _dot(m_g, g_lo))
        q_intra = (q * jnp.exp(d_q)).astype(BF16)
        k_intra = (kk * jnp.exp(-d_q)).astype(BF16)
        k_state = (kk * jnp.exp(d_k)).astype(BF16)
        q_inter = (q * e_g).astype(BF16)
        steps = [slice(n * step, (n + 1) * step) for n in range(sub // step)]
        scores = [lax.dot_general(q_intra[sl], k_intra[sl], _NT, preferred_element_type=F32)
                  for sl in steps]
        upd_t = [lax.dot_general(v[sl], k_state[sl], _TN, preferred_element_type=F32)
                 for sl in steps]
        intra = [_dot(jnp.where(tril, sc, 0.0).astype(BF16), v[sl])
                 for sc, sl in zip(scores, steps)]
        state_t = state_ref[...]
        inter = []
        for gi in range(sub // LANES):
            q_grp = q_inter[gi * LANES:(gi + 1) * LANES]
            o_t = None
            for m in range(per_grp):
                n = gi * per_grp + m
                res = lax.dot_general(state_t.astype(BF16), q_grp, _NT, preferred_element_type=F32)
                o_t = res if o_t is None else jnp.where(lane_step == m, res, o_t)
                state_t = state_t * e_g[(n + 1) * step - 1:(n + 1) * step] + upd_t[n]
            inter.append(o_t.T)
        state_ref[...] = state_t
        o = jnp.concatenate(intra, axis=0) + jnp.concatenate(inter, axis=0)
        o = o * lax.rsqrt(jnp.mean(o * o, axis=-1, keepdims=True) + RMS_EPS) * ng
        o = o * jax.nn.silu(gate_ref[0, rows, :].astype(F32))
        o_ref[0, rows, :] = o.astype(BF16)
        return carry

    lax.fori_loop(0, q_ref.shape[1] // sub, body, 0, unroll=True)


def _hgrn2(proj3, lb_logits, norm_g, *, layer, ts=4096):
    b, s, _ = proj3.shape
    ts = min(ts, s)
    base = 3 * SB_WIDTH // LANES
    depth1 = lb_logits.shape[0]

    def col(k):
        return pl.BlockSpec((1, ts, LANES), lambda bi, h, si: (bi, si, base + k * HG_HEADS + h))

    return pl.pallas_call(
        functools.partial(_hg_kernel, layer=layer),
        grid=(b, HG_HEADS, s // ts),
        in_specs=[col(0), col(1), col(2), col(3),
                  pl.BlockSpec((depth1, LANES), lambda bi, h, si: (0, h)),
                  pl.BlockSpec((1, 1, LANES), lambda bi, h, si: (h, 0, 0))],
        out_specs=pl.BlockSpec((1, ts, LANES), lambda bi, h, si: (bi, si, h)),
        out_shape=jax.ShapeDtypeStruct((b, s, HG_WIDTH), BF16),
        scratch_shapes=[pltpu.VMEM((HG_HEAD_DIM, HG_HEAD_DIM), F32)],
        compiler_params=pltpu.CompilerParams(
            dimension_semantics=("arbitrary", "arbitrary", "arbitrary"),
            vmem_limit_bytes=VMEM_LIMIT),
        name="hgrn2",
    )(proj3, proj3, proj3, proj3, lb_logits, norm_g.reshape(HG_HEADS, 1, HG_HEAD_DIM))


def _proj_mlp_kernel(ma_ref, mb_ref, h_ref, wp_ref, bp_ref, g_ref, w1_ref, w2_ref, gf_ref,
                     o_ref, *, tf, final_norm):
    half = ma_ref.shape[1]
    h1 = (h_ref[...] + _dot(ma_ref[...], wp_ref[:half, :]) + _dot(mb_ref[...], wp_ref[half:, :])
          + bp_ref[...])
    u = _rms(h1, g_ref[...]).astype(BF16)
    acc = h1
    for c in range(w1_ref.shape[1] // tf):
        a = jnp.maximum(_dot(u, w1_ref[:, c * tf:(c + 1) * tf]), 0.0)
        acc = acc + _dot((a * a).astype(BF16), w2_ref[c * tf:(c + 1) * tf, :])
    if final_norm:
        acc = _rms(acc, gf_ref[...])
    o_ref[...] = acc


def _proj_mlp(ma, mb, cols_a, cols_b, h2d, wp, bp, g, w1, w2, gf, *, final_norm, tm=1024, tf=1024):
    t, d = h2d.shape
    half = d // 2
    dff = w1.shape[1]
    tm = min(tm, t)
    return pl.pallas_call(
        functools.partial(_proj_mlp_kernel, tf=tf, final_norm=final_norm),
        grid=(t // tm,),
        in_specs=[pl.BlockSpec((tm, half), lambda i: (i, cols_a)),
                  pl.BlockSpec((tm, half), lambda i: (i, cols_b)),
                  pl.BlockSpec((tm, d), lambda i: (i, 0)),
                  _const_spec((d, d)), _const_spec((1, d)), _const_spec((1, d)),
                  _const_spec((d, dff)), _const_spec((dff, d)), _const_spec((1, d))],
        out_specs=pl.BlockSpec((tm, d), lambda i: (i, 0)),
        out_shape=jax.ShapeDtypeStruct((t, d), F32),
        compiler_params=pltpu.CompilerParams(dimension_semantics=("arbitrary",),
                                             vmem_limit_bytes=VMEM_LIMIT),
        name="proj_mlp_final" if final_norm else "proj_mlp",
    )(ma, mb, h2d, wp, bp, g, w1, w2, gf)


def _conv_kernel(h_ref, g_ref, wg_ref, bg_ref, wdw_ref, bdw_ref, lng_ref, lnb_ref, o_ref,
                 abuf_ref, y_ref, *, rc):
    si = pl.program_id(1)
    ts, d = h_ref.shape[1], h_ref.shape[2]
    hist = CONV_HIST

    @pl.when(si == 0)
    def _():
        abuf_ref[0:hist, :] = jnp.zeros((hist, d), F32)

    @pl.when(si > 0)
    def _():
        abuf_ref[0:hist, :] = abuf_ref[ts:ts + hist, :]

    u = _rms(h_ref[0], g_ref[...]).astype(BF16)
    lin_a = _dot(u, wg_ref[:, :d]) + bg_ref[:, :d]
    lin_b = _dot(u, wg_ref[:, d:]) + bg_ref[:, d:]
    abuf_ref[hist:, :] = lin_a * jax.nn.sigmoid(lin_b)

    n_win = rc + hist
    pr = lax.broadcasted_iota(jnp.int32, ((SUBLANES - 1) * n_win, n_win), 0)
    pc = lax.broadcasted_iota(jnp.int32, ((SUBLANES - 1) * n_win, n_win), 1)
    shift_mat = jnp.where(pr % n_win - pc == pr // n_win + 1, 1.0, 0.0).astype(BF16)
    for l0 in range(0, d, 2 * LANES):
        for r0 in range(0, ts, rc):
            win2 = abuf_ref[r0:r0 + n_win, l0:l0 + 2 * LANES]
            shifted2 = _dot(shift_mat, win2.astype(BF16))
            for half in range(2):
                lanes = slice(l0 + half * LANES, l0 + (half + 1) * LANES)
                cols = slice(half * LANES, (half + 1) * LANES)
                acc = jnp.broadcast_to(bdw_ref[:, lanes], (rc, LANES))
                for sft in range(SUBLANES):
                    moved = (win2[:, cols] if sft == 0
                             else shifted2[(sft - 1) * n_win:sft * n_win, cols])
                    for dly in range(sft, CONV_WIDTH, SUBLANES):
                        tap = CONV_WIDTH - 1 - dly
                        start = hist - (dly - sft)
                        acc = acc + wdw_ref[tap:tap + 1, lanes] * moved[start:start + rc]
                y_ref[r0:r0 + rc, lanes] = acc

    y = y_ref[...]
    mu = jnp.mean(y, axis=-1, keepdims=True)
    yc = y - mu
    var = jnp.mean(yc * yc, axis=-1, keepdims=True)
    yn = yc * lax.rsqrt(var + LN_EPS) * lng_ref[...] + lnb_ref[...]
    o_ref[0] = (yn * jax.nn.sigmoid(yn)).astype(BF16)


def _conv_front(h3, g, wg, bg, wdw, bdw, lng, lnb, *, ts=1024, rc=64):
    b, s, d = h3.shape
    ts = min(ts, s)
    return pl.pallas_call(
        functools.partial(_conv_kernel, rc=rc),
        grid=(b, s // ts),
        in_specs=[pl.BlockSpec((1, ts, d), lambda bi, si: (bi, si, 0)),
                  _const_spec((1, d)), _const_spec((d, 2 * d)), _const_spec((1, 2 * d)),
                  _const_spec((CONV_WIDTH, d)), _const_spec((1, d)), _const_spec((1, d)),
                  _const_spec((1, d))],
        out_specs=pl.BlockSpec((1, ts, d), lambda bi, si: (bi, si, 0)),
        out_shape=jax.ShapeDtypeStruct((b, s, d), BF16),
        scratch_shapes=[pltpu.VMEM((ts + CONV_HIST, d), F32), pltpu.VMEM((ts, d), F32)],
        compiler_params=pltpu.CompilerParams(dimension_semantics=("arbitrary", "arbitrary"),
                                             vmem_limit_bytes=VMEM_LIMIT),
        name="conv_front",
    )(h3, g, wg, bg, wdw, bdw, lng, lnb)


def kernel(x, norm_mix_g, norm_ffn_g, w_in_ab, w_out_ab, hg_lb_logits, hg_norm_g, conv_w_glu,
           conv_b_glu, conv_w_dw, conv_b_dw, conv_ln_g, conv_ln_b, conv_w_pw, conv_b_pw,
           w_ff1, w_ff2, final_norm_g):
    b, s, d = x.shape
    depth = norm_mix_g.shape[0]
    row = lambda a: a.reshape(1, -1).astype(F32)
    zeros_d = jnp.zeros((1, d), F32)
    h = x.reshape(b * s, d)
    for layer in range(depth):
        j = layer // 2
        last = layer == depth - 1
        if layer % 2 == 0:
            q_scale = SB_HEAD_DIM ** -0.5 * LOG2E
            scale = jnp.where(jnp.arange(w_in_ab.shape[2]) < SB_WIDTH, q_scale, 1.0)
            w_in = (w_in_ab[j] * scale).astype(BF16)
            proj = _inproj(h, row(norm_mix_g[layer]), w_in).reshape(b, s, -1)
            o_sb = _sb_attention(proj).reshape(b * s, SB_WIDTH)
            o_hg = _hgrn2(proj, hg_lb_logits.astype(F32), hg_norm_g[j].astype(F32),
                          layer=layer).reshape(b * s, HG_WIDTH)
            mix = (o_sb, o_hg, 0, 0)
            wp, bp = w_out_ab[j].astype(BF16), zeros_d
        else:
            y = _conv_front(h.reshape(b, s, d), row(norm_mix_g[layer]),
                            conv_w_glu[j].astype(BF16), row(conv_b_glu[j]),
                            conv_w_dw[j].astype(F32), row(conv_b_dw[j]), row(conv_ln_g[j]),
                            row(conv_ln_b[j])).reshape(b * s, d)
            mix = (y, y, 0, 1)
            wp, bp = conv_w_pw[j].astype(BF16), row(conv_b_pw[j])
        h = _proj_mlp(*mix, h, wp, bp, row(norm_ffn_g[layer]), w_ff1[layer].astype(BF16),
                      w_ff2[layer].astype(BF16), row(final_norm_g), final_norm=last)
    return h.reshape(b, s, d)
```

```python
import functools
import math

import jax
import jax.numpy as jnp
from jax import lax
from jax.experimental import pallas as pl
from jax.experimental.pallas import tpu as pltpu

F32 = jnp.float32
BF16 = jnp.bfloat16

RMS_EPS = 1e-6
LN_EPS = 1e-5

SB_HEADS = 8
SB_HEAD_DIM = 64
SB_WIDTH = SB_HEADS * SB_HEAD_DIM
HG_HEADS = 4
HG_HEAD_DIM = 128
HG_WIDTH = HG_HEADS * HG_HEAD_DIM
CONV_WIDTH = 31

LANES = 128
SUBLANES = 8
LOG2E = math.log2(math.e)
SB_BLOCK = 128
SB_HALF = SB_BLOCK // 2
SB_UNDERFLOW_LOG2 = -92.0 * LOG2E
HG_STEP = 32
HG_SUB = 256
CONV_HIST = 32
GLU_PIECES = 4
VMEM_LIMIT = 56 * 1024 * 1024

_NT = (((1,), (1,)), ((), ()))
_TN = (((0,), (0,)), ((), ()))


def _rms(x, g):
    return x * lax.rsqrt(jnp.mean(x * x, axis=-1, keepdims=True) + RMS_EPS) * g


def _dot(a, b):
    return jnp.dot(a, b, preferred_element_type=F32)


def _const_spec(shape):
    nd = len(shape)
    return pl.BlockSpec(shape, lambda *_: (0,) * nd, pipeline_mode=pl.Buffered(1))


def _inproj_kernel(x_ref, g_ref, w_ref, o_ref, *, tn):
    u = _rms(x_ref[...], g_ref[...]).astype(BF16)
    for n in range(w_ref.shape[1] // tn):
        cols = slice(n * tn, (n + 1) * tn)
        o_ref[:, cols] = _dot(u, w_ref[:, cols]).astype(BF16)


def _inproj(x2d, g, w, *, tm=1024, tn=512):
    t, d = x2d.shape
    n = w.shape[1]
    tm = min(tm, t)
    return pl.pallas_call(
        functools.partial(_inproj_kernel, tn=tn),
        grid=(t // tm,),
        in_specs=[pl.BlockSpec((tm, d), lambda i: (i, 0)),
                  _const_spec((1, d)),
                  _const_spec((d, n))],
        out_specs=pl.BlockSpec((tm, n), lambda i: (i, 0)),
        out_shape=jax.ShapeDtypeStruct((t, n), BF16),
        compiler_params=pltpu.CompilerParams(dimension_semantics=("arbitrary",),
                                             vmem_limit_bytes=VMEM_LIMIT),
        name="inproj",
    )(x2d, g, w)


def _sb_kernel(q_ref, k_ref, v_ref, o_ref, r_ref, acc_ref):
    blk = SB_BLOCK
    n_sub = q_ref.shape[1] // blk
    qi0 = pl.program_id(1) * n_sub
    pairs = q_ref.shape[2] // LANES
    row = lax.broadcasted_iota(jnp.int32, (blk, blk), 0)
    col = lax.broadcasted_iota(jnp.int32, (blk, blk), 1)
    causal = col < row
    trow = lax.broadcasted_iota(jnp.int32, (blk, 2 * blk), 0)
    tcol = lax.broadcasted_iota(jnp.int32, (blk, 2 * blk), 1)
    tri_ones = jnp.where((trow > tcol) | (tcol >= blk), 1.0, 0.0).astype(BF16)

    def split(x):
        lo = lax.broadcasted_iota(jnp.int32, x.shape, 1) < SB_HEAD_DIM
        zero = jnp.zeros_like(x)
        return (jnp.where(lo, x, zero), jnp.where(lo, zero, x))

    def sweep(groups, fresh):
        units = [(p, groups) for p in range(pairs)]
        z_of, mid_of, vs_of = {}, {}, {}

        def stage_scores(u):
            p, grps = units[u]
            lanes = slice(p * LANES, (p + 1) * LANES)
            vs_of[u], z_of[u] = [], []
            for q_rows, windows, _ in grps:
                q2 = jnp.concatenate(split(q_ref[0, q_rows, lanes]), axis=0)
                vs_of[u].append([v_ref[0, w, lanes] for w in windows])
                z_of[u].append([lax.dot_general(q2, k_ref[0, w, lanes], _NT,
                                                preferred_element_type=F32) for w in windows])

        def stage_sums(u):
            _, grps = units[u]
            log_betas, log_keeps = [], []
            for gi, (_, windows, mask) in enumerate(grps):
                mask2 = None if mask is None else jnp.concatenate([mask, mask], axis=0)
                for j in range(len(windows)):
                    z = z_of[u][gi][j]
                    log_beta = jnp.minimum(z, 0.0) - jnp.log2(1.0 + jnp.exp2(-jnp.abs(z)))
                    log_keep = log_beta - z
                    if mask2 is not None and j == 0:
                        log_keep = jnp.where(mask2, log_keep, 0.0)
                    log_betas.append(log_beta)
                    log_keeps.append(log_keep.astype(BF16))
            sums = _dot(jnp.concatenate(log_keeps, axis=0), tri_ones)
            mid_of[u], at = [], 0
            for log_beta in log_betas:
                n = log_beta.shape[0]
                mid_of[u].append((log_beta, sums[at:at + n, :blk], sums[at:at + n, blk:]))
                at += n
            del z_of[u]

        def stage_values(u):
            p, grps = units[u]
            lanes = slice(p * LANES, (p + 1) * LANES)
            mids = iter(mid_of[u])
            for gi, (q_rows, windows, mask) in enumerate(grps):
                n = q_rows.stop - q_rows.start
                mask2 = None if mask is None else jnp.concatenate([mask, mask], axis=0)
                r = None if fresh else jnp.concatenate(
                    [r_ref[2 * p, q_rows, :], r_ref[2 * p + 1, q_rows, :]], axis=0)
                pv2 = None
                for j in range(len(windows)):
                    log_beta, later, total = next(mids)
                    e = log_beta + later
                    if r is not None:
                        e = e + r
                    w = jnp.exp2(e)
                    if mask2 is not None and j == 0:
                        w = jnp.where(mask2, w, 0.0)
                    r = total if r is None else r + total
                    part = _dot(w.astype(BF16), vs_of[u][gi][j])
                    pv2 = part if pv2 is None else pv2 + part
                r_ref[2 * p, q_rows, :] = r[:n]
                r_ref[2 * p + 1, q_rows, :] = r[n:]
                lo = lax.broadcasted_iota(jnp.int32, (n, LANES), 1) < SB_HEAD_DIM
                pv = jnp.where(lo, pv2[:n], pv2[n:])
                if fresh:
                    acc_ref[q_rows, lanes] = pv
                else:
                    acc_ref[q_rows, lanes] += pv
            del mid_of[u], vs_of[u]

        for t in range(len(units) + 2):
            if t < len(units):
                stage_scores(t)
            if 0 <= t - 1 < len(units):
                stage_sums(t - 1)
            if 0 <= t - 2 < len(units):
                stage_values(t - 2)

    def key_block(kb):
        return pl.ds(pl.multiple_of(kb * blk, blk), blk)

    def window_sweep(first_sub):
        win = 2 * blk
        wr = lax.broadcasted_iota(jnp.int32, (win, win), 0)
        wc = lax.broadcasted_iota(jnp.int32, (win, win), 1)
        tri = jnp.where(wr > wc, 1.0, 0.0).astype(BF16)
        hr = lax.broadcasted_iota(jnp.int32, (2 * SB_HALF, blk), 0) % SB_HALF
        hc = lax.broadcasted_iota(jnp.int32, (2 * SB_HALF, blk), 1)
        newest_ok = hc < hr + SB_HALF
        halves = [(slice(first, first + SB_HALF),
                   pl.ds(pl.multiple_of(qi0 * blk + first + SB_HALF - win, SB_HALF), win))
                  for first in range(first_sub * blk, n_sub * blk, SB_HALF)]
        z_of, mid_of, v_of = {}, {}, {}
        worst = [jnp.full((1, 1), -jnp.inf, F32)]

        def masked(x):
            return jnp.concatenate([x[:, :blk], jnp.where(newest_ok, x[:, blk:], 0.0)], axis=1)

        def stage_scores(p):
            lanes = slice(p * LANES, (p + 1) * LANES)
            z_of[p], v_of[p] = [], []
            for q_rows, keys in halves:
                q2 = jnp.concatenate(split(q_ref[0, q_rows, lanes]), axis=0)
                v_of[p].append(v_ref[0, keys, lanes])
                z_of[p].append(lax.dot_general(q2, k_ref[0, keys, lanes], _NT,
                                               preferred_element_type=F32))

        def stage_sums(p):
            log_betas, log_keeps = [], []
            for z in z_of[p]:
                log_beta = jnp.minimum(z, 0.0) - jnp.log2(1.0 + jnp.exp2(-jnp.abs(z)))
                log_keep = masked(log_beta - z)
                log_betas.append(log_beta)
                log_keeps.append(log_keep)
            stacked = jnp.concatenate(log_keeps, axis=0)
            later = _dot(stacked.astype(BF16), tri)
            totals = later[:, 0:1] + stacked[:, 0:1]
            worst[0] = jnp.maximum(worst[0], jnp.max(totals, axis=0, keepdims=True))
            n = 2 * SB_HALF
            mid_of[p] = [(lb, later[i * n:(i + 1) * n]) for i, lb in enumerate(log_betas)]
            del z_of[p]

        def stage_values(p):
            lanes = slice(p * LANES, (p + 1) * LANES)
            lo = lax.broadcasted_iota(jnp.int32, (SB_HALF, LANES), 1) < SB_HEAD_DIM
            for (q_rows, _), (log_beta, later), v in zip(halves, mid_of[p], v_of[p]):
                w = masked(jnp.exp2(log_beta + later))
                pv2 = _dot(w.astype(BF16), v)
                acc_ref[q_rows, lanes] = jnp.where(lo, pv2[:SB_HALF], pv2[SB_HALF:])
            del mid_of[p], v_of[p]

        for t in range(pairs + 2):
            if t < pairs:
                stage_scores(t)
            if 0 <= t - 1 < pairs:
                stage_sums(t - 1)
            if 0 <= t - 2 < pairs:
                stage_values(t - 2)
        r_ref[0, 0:SUBLANES, :] = jnp.broadcast_to(worst[0], (SUBLANES, blk))

    seq_start = qi0 < 2

    @pl.when(seq_start)
    def _():
        if n_sub > 2:
            window_sweep(2)
        else:
            r_ref[0, 0:SUBLANES, :] = jnp.full((SUBLANES, blk), -jnp.inf, F32)

    @pl.when(jnp.logical_not(seq_start))
    def _():
        window_sweep(0)

    redo = jnp.max(r_ref[0, 0:SUBLANES, :]) > SB_UNDERFLOW_LOG2

    for sub in range(n_sub):
        q_rows = slice(sub * blk, (sub + 1) * blk)

        @pl.when(jnp.logical_or(redo, seq_start) if sub < 2 else redo)
        def _():
            sweep([(q_rows, [key_block(qi0 + sub)], causal)], True)

            def cond(c):
                kb, m = c
                return jnp.logical_and(kb >= 0, m > SB_UNDERFLOW_LOG2)

            def body(c):
                kb, _ = c
                sweep([(q_rows, [key_block(kb)], None)], False)
                return kb - 1, jnp.max(r_ref[:, q_rows, :])

            lax.while_loop(cond, body, (qi0 + sub - 1, jnp.max(r_ref[:, q_rows, :])))

    o_ref[0] = acc_ref[...].astype(BF16)


def _sb_attention(proj3, *, q_rows=4 * SB_BLOCK):
    b, s, _ = proj3.shape
    q_rows = min(q_rows, s)
    kv_spec = lambda c: pl.BlockSpec((1, s, SB_WIDTH), lambda bi, qi: (bi, 0, c))
    return pl.pallas_call(
        _sb_kernel,
        grid=(b, s // q_rows),
        in_specs=[pl.BlockSpec((1, q_rows, SB_WIDTH), lambda bi, qi: (bi, qi, 0)),
                  kv_spec(1), kv_spec(2)],
        out_specs=pl.BlockSpec((1, q_rows, SB_WIDTH), lambda bi, qi: (bi, qi, 0)),
        out_shape=jax.ShapeDtypeStruct((b, s, SB_WIDTH), BF16),
        scratch_shapes=[pltpu.VMEM((SB_HEADS, q_rows, SB_BLOCK), F32),
                        pltpu.VMEM((q_rows, SB_WIDTH), F32)],
        compiler_params=pltpu.CompilerParams(dimension_semantics=("arbitrary", "arbitrary"),
                                             vmem_limit_bytes=VMEM_LIMIT),
        name="sb_attention",
    )(proj3, proj3, proj3)


def _hg_kernel(q_ref, f_ref, i_ref, gate_ref, lbl_ref, ng_ref, o_ref, state_ref, *, layer):
    si = pl.program_id(2)
    step, sub = HG_STEP, HG_SUB
    half = step // 2

    @pl.when(si == 0)
    def _():
        state_ref[...] = jnp.zeros_like(state_ref)

    logits = lbl_ref[...]
    e = jnp.exp(logits - jnp.max(logits, axis=0, keepdims=True))
    lb = jnp.sum(e[:layer + 1], axis=0, keepdims=True) / jnp.sum(e, axis=0, keepdims=True)
    ng = ng_ref[0]

    r = lax.broadcasted_iota(jnp.int32, (sub, sub), 0)
    c = lax.broadcasted_iota(jnp.int32, (sub, sub), 1)
    same = (r // step) == (c // step)
    incl = jnp.where(same & (c <= r), 1.0, 0.0)
    to_mid = jnp.where(same & (c % step < half), 1.0, 0.0)
    m_q = (incl - to_mid).astype(BF16)
    m_k = jnp.where(same & (c > r), 1.0, 0.0).astype(BF16)
    m_g = incl.astype(BF16)
    tr = lax.broadcasted_iota(jnp.int32, (step, step), 0)
    tc = lax.broadcasted_iota(jnp.int32, (step, step), 1)
    tril = tr >= tc
    per_grp = LANES // step
    lane_step = lax.broadcasted_iota(jnp.int32, (HG_HEAD_DIM, LANES), 1) // step

    def body(j, carry):
        rows = pl.ds(pl.multiple_of(j * sub, sub), sub)
        q = q_ref[0, rows, :].astype(F32)
        f = lb + (1.0 - lb) * jax.nn.sigmoid(f_ref[0, rows, :].astype(F32))
        v = i_ref[0, rows, :]
        g = jnp.log(f)
        kk = 1.0 - f
        g_hi = g.astype(BF16)
        g_lo = (g - g_hi.astype(F32)).astype(BF16)
        d_q = _dot(m_q, g_hi) + _dot(m_q, g_lo)
        d_k = _dot(m_k, g_hi) + _dot(m_k, g_lo)
        e_g = jnp.exp(_dot(m_g, g_hi) + _dot(m_g, g_lo))
        q_intra = (q * jnp.exp(d_q)).astype(BF16)
        k_intra = (kk * jnp.exp(-d_q)).astype(BF16)
        k_state = (kk * jnp.exp(d_k)).astype(BF16)
        q_inter = (q * e_g).astype(BF16)
        steps = [slice(n * step, (n + 1) * step) for n in range(sub // step)]
        scores = [lax.dot_general(q_intra[sl], k_intra[sl], _NT, preferred_element_type=F32)
                  for sl in steps]
        upd_t = [lax.dot_general(v[sl], k_state[sl], _TN, preferred_element_type=F32)
                 for sl in steps]
        intra = [_dot(jnp.where(tril, sc, 0.0).astype(BF16), v[sl])
                 for sc, sl in zip(scores, steps)]
        state_t = state_ref[...]
        inter = []
        for gi in range(sub // LANES):
            q_grp = q_inter[gi * LANES:(gi + 1) * LANES]
            o_t = None
            for m in range(per_grp):
                n = gi * per_grp + m
                res = lax.dot_general(state_t.astype(BF16), q_grp, _NT, preferred_element_type=F32)
                o_t = res if o_t is None else jnp.where(lane_step == m, res, o_t)
                state_t = state_t * e_g[(n + 1) * step - 1:(n + 1) * step] + upd_t[n]
            inter.append(o_t.T)
        state_ref[...] = state_t
        o = jnp.concatenate(intra, axis=0) + jnp.concatenate(inter, axis=0)
        o = o * lax.rsqrt(jnp.mean(o * o, axis=-1, keepdims=True) + RMS_EPS) * ng
        o = o * jax.nn.silu(gate_ref[0, rows, :].astype(F32))
        o_ref[0, rows, :] = o.astype(BF16)
        return carry

    lax.fori_loop(0, q_ref.shape[1] // sub, body, 0, unroll=True)


def _hgrn2(proj3, lb_logits, norm_g, *, layer, ts=4096):
    b, s, _ = proj3.shape
    ts = min(ts, s)
    base = 3 * SB_WIDTH // LANES
    depth1 = lb_logits.shape[0]

    def col(k):
        return pl.BlockSpec((1, ts, LANES), lambda bi, h, si: (bi, si, base + k * HG_HEADS + h))

    return pl.pallas_call(
        functools.partial(_hg_kernel, layer=layer),
        grid=(b, HG_HEADS, s // ts),
        in_specs=[col(0), col(1), col(2), col(3),
                  pl.BlockSpec((depth1, LANES), lambda bi, h, si: (0, h)),
                  pl.BlockSpec((1, 1, LANES), lambda bi, h, si: (h, 0, 0))],
        out_specs=pl.BlockSpec((1, ts, LANES), lambda bi, h, si: (bi, si, h)),
        out_shape=jax.ShapeDtypeStruct((b, s, HG_WIDTH), BF16),
        scratch_shapes=[pltpu.VMEM((HG_HEAD_DIM, HG_HEAD_DIM), F32)],
        compiler_params=pltpu.CompilerParams(
            dimension_semantics=("arbitrary", "arbitrary", "arbitrary"),
            vmem_limit_bytes=VMEM_LIMIT),
        name="hgrn2",
    )(proj3, proj3, proj3, proj3, lb_logits, norm_g.reshape(HG_HEADS, 1, HG_HEAD_DIM))


def _proj_mlp_kernel(ma_ref, mb_ref, h_ref, wp_ref, bp_ref, g_ref, w1_ref, w2_ref, gf_ref,
                     o_ref, *, tf, final_norm):
    half = ma_ref.shape[1]
    h1 = (h_ref[...] + _dot(ma_ref[...], wp_ref[:half, :]) + _dot(mb_ref[...], wp_ref[half:, :])
          + bp_ref[...])
    u = _rms(h1, g_ref[...]).astype(BF16)
    acc = h1
    for c in range(w1_ref.shape[1] // tf):
        a = jnp.maximum(_dot(u, w1_ref[:, c * tf:(c + 1) * tf]), 0.0)
        acc = acc + _dot((a * a).astype(BF16), w2_ref[c * tf:(c + 1) * tf, :])
    if final_norm:
        acc = _rms(acc, gf_ref[...])
    o_ref[...] = acc


def _proj_mlp(ma, mb, cols_a, cols_b, h2d, wp, bp, g, w1, w2, gf, *, final_norm, tm=1024, tf=1024):
    t, d = h2d.shape
    half = d // 2
    dff = w1.shape[1]
    tm = min(tm, t)
    return pl.pallas_call(
        functools.partial(_proj_mlp_kernel, tf=tf, final_norm=final_norm),
        grid=(t // tm,),
        in_specs=[pl.BlockSpec((tm, half), lambda i: (i, cols_a)),
                  pl.BlockSpec((tm, half), lambda i: (i, cols_b)),
                  pl.BlockSpec((tm, d), lambda i: (i, 0)),
                  _const_spec((d, d)), _const_spec((1, d)), _const_spec((1, d)),
                  _const_spec((d, dff)), _const_spec((dff, d)), _const_spec((1, d))],
        out_specs=pl.BlockSpec((tm, d), lambda i: (i, 0)),
        out_shape=jax.ShapeDtypeStruct((t, d), F32),
        compiler_params=pltpu.CompilerParams(dimension_semantics=("arbitrary",),
                                             vmem_limit_bytes=VMEM_LIMIT),
        name="proj_mlp_final" if final_norm else "proj_mlp",
    )(ma, mb, h2d, wp, bp, g, w1, w2, gf)


def _conv_kernel(h_ref, g_ref, wg_ref, bg_ref, wdw_ref, bdw_ref, lng_ref, lnb_ref, o_ref,
                 abuf_ref, y_ref, *, rc):
    si = pl.program_id(1)
    ts, d = h_ref.shape[1], h_ref.shape[2]
    hist = CONV_HIST

    @pl.when(si == 0)
    def _():
        abuf_ref[0:hist, :] = jnp.zeros((hist, d), F32)

    @pl.when(si > 0)
    def _():
        abuf_ref[0:hist, :] = abuf_ref[ts:ts + hist, :]

    u = _rms(h_ref[0], g_ref[...]).astype(BF16)
    pair_w = 2 * LANES
    n_pairs = d // pair_w
    chunks = list(range(0, ts, rc))
    n_pieces = min(GLU_PIECES, len(chunks))
    per_piece = len(chunks) // n_pieces
    glu_rows = ts // n_pieces

    def glu_piece(p, k):
        cols = slice(p * pair_w, (p + 1) * pair_w)
        gate = slice(d + p * pair_w, d + (p + 1) * pair_w)
        rows = slice(k * glu_rows, (k + 1) * glu_rows)
        lin_a = _dot(u[rows], wg_ref[:, cols]) + bg_ref[:, cols]
        lin_b = _dot(u[rows], wg_ref[:, gate]) + bg_ref[:, gate]
        abuf_ref[hist + k * glu_rows:hist + (k + 1) * glu_rows, cols] = lin_a * jax.nn.sigmoid(lin_b)

    for k in range(n_pieces):
        glu_piece(0, k)

    n_win = rc + hist
    pr = lax.broadcasted_iota(jnp.int32, ((SUBLANES - 1) * n_win, n_win), 0)
    pc = lax.broadcasted_iota(jnp.int32, ((SUBLANES - 1) * n_win, n_win), 1)
    shift_mat = jnp.where(pr % n_win - pc == pr // n_win + 1, 1.0, 0.0).astype(BF16)
    for l0 in range(0, d, pair_w):
        p = l0 // pair_w
        for ci, r0 in enumerate(chunks):
            if p + 1 < n_pairs and ci % per_piece == 0 and ci // per_piece < n_pieces:
                glu_piece(p + 1, ci // per_piece)
            win2 = abuf_ref[r0:r0 + n_win, l0:l0 + 2 * LANES]
            shifted2 = _dot(shift_mat, win2.astype(BF16))
            for half in range(2):
                lanes = slice(l0 + half * LANES, l0 + (half + 1) * LANES)
                cols = slice(half * LANES, (half + 1) * LANES)
                acc = jnp.broadcast_to(bdw_ref[:, lanes], (rc, LANES))
                for sft in range(SUBLANES):
                    moved = (win2[:, cols] if sft == 0
                             else shifted2[(sft - 1) * n_win:sft * n_win, cols])
                    for dly in range(sft, CONV_WIDTH, SUBLANES):
                        tap = CONV_WIDTH - 1 - dly
                        start = hist - (dly - sft)
                        acc = acc + wdw_ref[tap:tap + 1, lanes] * moved[start:start + rc]
                y_ref[r0:r0 + rc, lanes] = acc

    y = y_ref[...]
    mu = jnp.mean(y, axis=-1, keepdims=True)
    yc = y - mu
    var = jnp.mean(yc * yc, axis=-1, keepdims=True)
    yn = yc * lax.rsqrt(var + LN_EPS) * lng_ref[...] + lnb_ref[...]
    o_ref[0] = (yn * jax.nn.sigmoid(yn)).astype(BF16)


def _conv_front(h3, g, wg, bg, wdw, bdw, lng, lnb, *, ts=1024, rc=64):
    b, s, d = h3.shape
    ts = min(ts, s)
    return pl.pallas_call(
        functools.partial(_conv_kernel, rc=rc),
        grid=(b, s // ts),
        in_specs=[pl.BlockSpec((1, ts, d), lambda bi, si: (bi, si, 0)),
                  _const_spec((1, d)), _const_spec((d, 2 * d)), _const_spec((1, 2 * d)),
                  _const_spec((CONV_WIDTH, d)), _const_spec((1, d)), _const_spec((1, d)),
                  _const_spec((1, d))],
        out_specs=pl.BlockSpec((1, ts, d), lambda bi, si: (bi, si, 0)),
        out_shape=jax.ShapeDtypeStruct((b, s, d), BF16),
        scratch_shapes=[pltpu.VMEM((ts + CONV_HIST, d), F32), pltpu.VMEM((ts, d), F32)],
        compiler_params=pltpu.CompilerParams(dimension_semantics=("arbitrary", "arbitrary"),
                                             vmem_limit_bytes=VMEM_LIMIT),
        name="conv_front",
    )(h3, g, wg, bg, wdw, bdw, lng, lnb)


def kernel(x, norm_mix_g, norm_ffn_g, w_in_ab, w_out_ab, hg_lb_logits, hg_norm_g, conv_w_glu,
           conv_b_glu, conv_w_dw, conv_b_dw, conv_ln_g, conv_ln_b, conv_w_pw, conv_b_pw,
           w_ff1, w_ff2, final_norm_g):
    b, s, d = x.shape
    depth = norm_mix_g.shape[0]
    row = lambda a: a.reshape(1, -1).astype(F32)
    zeros_d = jnp.zeros((1, d), F32)
    h = x.reshape(b * s, d)
    for layer in range(depth):
        j = layer // 2
        last = layer == depth - 1
        if layer % 2 == 0:
            q_scale = SB_HEAD_DIM ** -0.5 * LOG2E
            scale = jnp.where(jnp.arange(w_in_ab.shape[2]) < SB_WIDTH, q_scale, 1.0)
            w_in = (w_in_ab[j] * scale).astype(BF16)
            proj = _inproj(h, row(norm_mix_g[layer]), w_in).reshape(b, s, -1)
            o_sb = _sb_attention(proj).reshape(b * s, SB_WIDTH)
            o_hg = _hgrn2(proj, hg_lb_logits.astype(F32), hg_norm_g[j].astype(F32),
                          layer=layer).reshape(b * s, HG_WIDTH)
            mix = (o_sb, o_hg, 0, 0)
            wp, bp = w_out_ab[j].astype(BF16), zeros_d
        else:
            y = _conv_front(h.reshape(b, s, d), row(norm_mix_g[layer]),
                            conv_w_glu[j].astype(BF16), row(conv_b_glu[j]),
                            conv_w_dw[j].astype(F32), row(conv_b_dw[j]), row(conv_ln_g[j]),
                            row(conv_ln_b[j])).reshape(b * s, d)
            mix = (y, y, 0, 1)
            wp, bp = conv_w_pw[j].astype(BF16), row(conv_b_pw[j])
        h = _proj_mlp(*mix, h, wp, bp, row(norm_ffn_g[layer]), w_ff1[layer].astype(BF16),
                      w_ff2[layer].astype(BF16), row(final_norm_g), final_norm=last)
    return h.reshape(b, s, d)
```

```python
import functools
import math

import jax
import jax.numpy as jnp
from jax import lax
from jax.experimental import pallas as pl
from jax.experimental.pallas import tpu as pltpu

F32 = jnp.float32
BF16 = jnp.bfloat16

RMS_EPS = 1e-6
LN_EPS = 1e-5

SB_HEADS = 8
SB_HEAD_DIM = 64
SB_WIDTH = SB_HEADS * SB_HEAD_DIM
HG_HEADS = 4
HG_HEAD_DIM = 128
HG_WIDTH = HG_HEADS * HG_HEAD_DIM
CONV_WIDTH = 31

LANES = 128
SUBLANES = 8
LOG2E = math.log2(math.e)
SB_BLOCK = 128
SB_HALF = SB_BLOCK // 2
SB_UNDERFLOW_LOG2 = -92.0 * LOG2E
HG_STEP = 32
HG_SUB = 256
CONV_HIST = 32
VMEM_LIMIT = 56 * 1024 * 1024

_NT = (((1,), (1,)), ((), ()))
_TN = (((0,), (0,)), ((), ()))


def _rms(x, g):
    return x * lax.rsqrt(jnp.mean(x * x, axis=-1, keepdims=True) + RMS_EPS) * g


def _dot(a, b):
    return jnp.dot(a, b, preferred_element_type=F32)


def _const_spec(shape):
    nd = len(shape)
    return pl.BlockSpec(shape, lambda *_: (0,) * nd, pipeline_mode=pl.Buffered(1))


def _inproj_kernel(x_ref, g_ref, w_ref, o_ref, *, tn):
    u = _rms(x_ref[...], g_ref[...]).astype(BF16)
    for n in range(w_ref.shape[1] // tn):
        cols = slice(n * tn, (n + 1) * tn)
        o_ref[:, cols] = _dot(u, w_ref[:, cols]).astype(BF16)


def _inproj(x2d, g, w, *, tm=1024, tn=512):
    t, d = x2d.shape
    n = w.shape[1]
    tm = min(tm, t)
    return pl.pallas_call(
        functools.partial(_inproj_kernel, tn=tn),
        grid=(t // tm,),
        in_specs=[pl.BlockSpec((tm, d), lambda i: (i, 0)),
                  _const_spec((1, d)),
                  _const_spec((d, n))],
        out_specs=pl.BlockSpec((tm, n), lambda i: (i, 0)),
        out_shape=jax.ShapeDtypeStruct((t, n), BF16),
        compiler_params=pltpu.CompilerParams(dimension_semantics=("arbitrary",),
                                             vmem_limit_bytes=VMEM_LIMIT),
        name="inproj",
    )(x2d, g, w)


def _sb_kernel(q_ref, k_ref, v_ref, o_ref, r_ref, acc_ref):
    blk = SB_BLOCK
    n_sub = q_ref.shape[1] // blk
    qi0 = pl.program_id(1) * n_sub
    pairs = q_ref.shape[2] // LANES
    row = lax.broadcasted_iota(jnp.int32, (blk, blk), 0)
    col = lax.broadcasted_iota(jnp.int32, (blk, blk), 1)
    causal = col < row
    trow = lax.broadcasted_iota(jnp.int32, (blk, 2 * blk), 0)
    tcol = lax.broadcasted_iota(jnp.int32, (blk, 2 * blk), 1)
    tri_ones = jnp.where((trow > tcol) | (tcol >= blk), 1.0, 0.0).astype(BF16)

    def split(x):
        lo = lax.broadcasted_iota(jnp.int32, x.shape, 1) < SB_HEAD_DIM
        zero = jnp.zeros_like(x)
        return (jnp.where(lo, x, zero), jnp.where(lo, zero, x))

    def sweep(groups, fresh):
        units = [(p, groups) for p in range(pairs)]
        z_of, mid_of, vs_of = {}, {}, {}

        def stage_scores(u):
            p, grps = units[u]
            lanes = slice(p * LANES, (p + 1) * LANES)
            vs_of[u], z_of[u] = [], []
            for q_rows, windows, _ in grps:
                q2 = jnp.concatenate(split(q_ref[0, q_rows, lanes]), axis=0)
                vs_of[u].append([v_ref[0, w, lanes] for w in windows])
                z_of[u].append([lax.dot_general(q2, k_ref[0, w, lanes], _NT,
                                                preferred_element_type=F32) for w in windows])

        def stage_sums(u):
            _, grps = units[u]
            log_betas, log_keeps = [], []
            for gi, (_, windows, mask) in enumerate(grps):
                mask2 = None if mask is None else jnp.concatenate([mask, mask], axis=0)
                for j in range(len(windows)):
                    z = z_of[u][gi][j]
                    log_beta = jnp.minimum(z, 0.0) - jnp.log2(1.0 + jnp.exp2(-jnp.abs(z)))
                    log_keep = log_beta - z
                    if mask2 is not None and j == 0:
                        log_keep = jnp.where(mask2, log_keep, 0.0)
                    log_betas.append(log_beta)
                    log_keeps.append(log_keep.astype(BF16))
            sums = _dot(jnp.concatenate(log_keeps, axis=0), tri_ones)
            mid_of[u], at = [], 0
            for log_beta in log_betas:
                n = log_beta.shape[0]
                mid_of[u].append((log_beta, sums[at:at + n, :blk], sums[at:at + n, blk:]))
                at += n
            del z_of[u]

        def stage_values(u):
            p, grps = units[u]
            lanes = slice(p * LANES, (p + 1) * LANES)
            mids = iter(mid_of[u])
            for gi, (q_rows, windows, mask) in enumerate(grps):
                n = q_rows.stop - q_rows.start
                mask2 = None if mask is None else jnp.concatenate([mask, mask], axis=0)
                r = None if fresh else jnp.concatenate(
                    [r_ref[2 * p, q_rows, :], r_ref[2 * p + 1, q_rows, :]], axis=0)
                pv2 = None
                for j in range(len(windows)):
                    log_beta, later, total = next(mids)
                    e = log_beta + later
                    if r is not None:
                        e = e + r
                    w = jnp.exp2(e)
                    if mask2 is not None and j == 0:
                        w = jnp.where(mask2, w, 0.0)
                    r = total if r is None else r + total
                    part = _dot(w.astype(BF16), vs_of[u][gi][j])
                    pv2 = part if pv2 is None else pv2 + part
                r_ref[2 * p, q_rows, :] = r[:n]
                r_ref[2 * p + 1, q_rows, :] = r[n:]
                lo = lax.broadcasted_iota(jnp.int32, (n, LANES), 1) < SB_HEAD_DIM
                pv = jnp.where(lo, pv2[:n], pv2[n:])
                if fresh:
                    acc_ref[q_rows, lanes] = pv
                else:
                    acc_ref[q_rows, lanes] += pv
            del mid_of[u], vs_of[u]

        for t in range(len(units) + 2):
            if t < len(units):
                stage_scores(t)
            if 0 <= t - 1 < len(units):
                stage_sums(t - 1)
            if 0 <= t - 2 < len(units):
                stage_values(t - 2)

    def key_block(kb):
        return pl.ds(pl.multiple_of(kb * blk, blk), blk)

    def window_sweep(first_sub):
        win = 2 * blk
        wr = lax.broadcasted_iota(jnp.int32, (win, win), 0)
        wc = lax.broadcasted_iota(jnp.int32, (win, win), 1)
        tri = jnp.where(wr > wc, 1.0, 0.0).astype(BF16)
        hr = lax.broadcasted_iota(jnp.int32, (2 * SB_HALF, blk), 0) % SB_HALF
        hc = lax.broadcasted_iota(jnp.int32, (2 * SB_HALF, blk), 1)
        newest_ok = hc < hr + SB_HALF
        halves = [(slice(first, first + SB_HALF),
                   pl.ds(pl.multiple_of(qi0 * blk + first + SB_HALF - win, SB_HALF), win))
                  for first in range(first_sub * blk, n_sub * blk, SB_HALF)]
        z_of, mid_of, v_of = {}, {}, {}
        worst = [jnp.full((1, 1), -jnp.inf, F32)]

        def masked(x):
            return jnp.concatenate([x[:, :blk], jnp.where(newest_ok, x[:, blk:], 0.0)], axis=1)

        def stage_scores(p):
            lanes = slice(p * LANES, (p + 1) * LANES)
            z_of[p], v_of[p] = [], []
            for q_rows, keys in halves:
                q2 = jnp.concatenate(split(q_ref[0, q_rows, lanes]), axis=0)
                v_of[p].append(v_ref[0, keys, lanes])
                z_of[p].append(lax.dot_general(q2, k_ref[0, keys, lanes], _NT,
                                               preferred_element_type=F32))

        def stage_sums(p):
            log_betas, log_keeps = [], []
            for z in z_of[p]:
                log_beta = jnp.minimum(z, 0.0) - jnp.log2(1.0 + jnp.exp2(-jnp.abs(z)))
                log_keep = masked(log_beta - z)
                log_betas.append(log_beta)
                log_keeps.append(log_keep)
            stacked = jnp.concatenate(log_keeps, axis=0)
            later = _dot(stacked.astype(BF16), tri)
            totals = later[:, 0:1] + stacked[:, 0:1]
            worst[0] = jnp.maximum(worst[0], jnp.max(totals, axis=0, keepdims=True))
            n = 2 * SB_HALF
            mid_of[p] = [(lb, later[i * n:(i + 1) * n]) for i, lb in enumerate(log_betas)]
            del z_of[p]

        def stage_values(p):
            lanes = slice(p * LANES, (p + 1) * LANES)
            lo = lax.broadcasted_iota(jnp.int32, (SB_HALF, LANES), 1) < SB_HEAD_DIM
            for (q_rows, _), (log_beta, later), v in zip(halves, mid_of[p], v_of[p]):
                w = masked(jnp.exp2(log_beta + later))
                pv2 = _dot(w.astype(BF16), v)
                acc_ref[q_rows, lanes] = jnp.where(lo, pv2[:SB_HALF], pv2[SB_HALF:])
            del mid_of[p], v_of[p]

        for t in range(pairs + 2):
            if t < pairs:
                stage_scores(t)
            if 0 <= t - 1 < pairs:
                stage_sums(t - 1)
            if 0 <= t - 2 < pairs:
                stage_values(t - 2)
        r_ref[0, 0:SUBLANES, :] = jnp.broadcast_to(worst[0], (SUBLANES, blk))

    seq_start = qi0 < 2

    @pl.when(seq_start)
    def _():
        if n_sub > 2:
            window_sweep(2)
        else:
            r_ref[0, 0:SUBLANES, :] = jnp.full((SUBLANES, blk), -jnp.inf, F32)

    @pl.when(jnp.logical_not(seq_start))
    def _():
        window_sweep(0)

    redo = jnp.max(r_ref[0, 0:SUBLANES, :]) > SB_UNDERFLOW_LOG2

    for sub in range(n_sub):
        q_rows = slice(sub * blk, (sub + 1) * blk)

        @pl.when(jnp.logical_or(redo, seq_start) if sub < 2 else redo)
        def _():
            sweep([(q_rows, [key_block(qi0 + sub)], causal)], True)

            def cond(c):
                kb, m = c
                return jnp.logical_and(kb >= 0, m > SB_UNDERFLOW_LOG2)

            def body(c):
                kb, _ = c
                sweep([(q_rows, [key_block(kb)], None)], False)
                return kb - 1, jnp.max(r_ref[:, q_rows, :])

            lax.while_loop(cond, body, (qi0 + sub - 1, jnp.max(r_ref[:, q_rows, :])))

    o_ref[0] = acc_ref[...].astype(BF16)


def _sb_attention(proj3, *, q_rows=4 * SB_BLOCK):
    b, s, _ = proj3.shape
    q_rows = min(q_rows, s)
    kv_spec = lambda c: pl.BlockSpec((1, s, SB_WIDTH), lambda bi, qi: (bi, 0, c))
    return pl.pallas_call(
        _sb_kernel,
        grid=(b, s // q_rows),
        in_specs=[pl.BlockSpec((1, q_rows, SB_WIDTH), lambda bi, qi: (bi, qi, 0)),
                  kv_spec(1), kv_spec(2)],
        out_specs=pl.BlockSpec((1, q_rows, SB_WIDTH), lambda bi, qi: (bi, qi, 0)),
        out_shape=jax.ShapeDtypeStruct((b, s, SB_WIDTH), BF16),
        scratch_shapes=[pltpu.VMEM((SB_HEADS, q_rows, SB_BLOCK), F32),
                        pltpu.VMEM((q_rows, SB_WIDTH), F32)],
        compiler_params=pltpu.CompilerParams(dimension_semantics=("arbitrary", "arbitrary"),
                                             vmem_limit_bytes=VMEM_LIMIT),
        name="sb_attention",
    )(proj3, proj3, proj3)


def _hg_kernel(q_ref, f_ref, i_ref, gate_ref, lbl_ref, ng_ref, o_ref, state_ref, *, layer):
    si = pl.program_id(1)
    step, sub = HG_STEP, HG_SUB
    half = step // 2
    heads = q_ref.shape[2] // LANES

    @pl.when(si == 0)
    def _():
        state_ref[...] = jnp.zeros_like(state_ref)

    logits = lbl_ref[...]
    e = jnp.exp(logits - jnp.max(logits, axis=0, keepdims=True))
    lb_all = jnp.sum(e[:layer + 1], axis=0, keepdims=True) / jnp.sum(e, axis=0, keepdims=True)
    ng_all = ng_ref[...]

    r = lax.broadcasted_iota(jnp.int32, (sub, sub), 0)
    c = lax.broadcasted_iota(jnp.int32, (sub, sub), 1)
    same = (r // step) == (c // step)
    incl = jnp.where(same & (c <= r), 1.0, 0.0)
    to_mid = jnp.where(same & (c % step < half), 1.0, 0.0)
    m_q = (incl - to_mid).astype(BF16)
    m_k = jnp.where(same & (c > r), 1.0, 0.0).astype(BF16)
    m_g = incl.astype(BF16)
    tr = lax.broadcasted_iota(jnp.int32, (step, step), 0)
    tc = lax.broadcasted_iota(jnp.int32, (step, step), 1)
    tril = tr >= tc
    per_grp = LANES // step
    lane_step = lax.broadcasted_iota(jnp.int32, (HG_HEAD_DIM, LANES), 1) // step

    steps = [slice(n * step, (n + 1) * step) for n in range(sub // step)]

    def prepare(h, rows):
        lanes = slice(h * LANES, (h + 1) * LANES)
        lb = lb_all[:, lanes]
        q = q_ref[0, rows, lanes].astype(F32)
        f = lb + (1.0 - lb) * jax.nn.sigmoid(f_ref[0, rows, lanes].astype(F32))
        v = i_ref[0, rows, lanes]
        g = jnp.log(f)
        kk = 1.0 - f
        g_hi = g.astype(BF16)
        g_lo = (g - g_hi.astype(F32)).astype(BF16)
        d_q = _dot(m_q, g_hi) + _dot(m_q, g_lo)
        d_k = _dot(m_k, g_hi) + _dot(m_k, g_lo)
        e_g = jnp.exp(_dot(m_g, g_hi) + _dot(m_g, g_lo))
        q_intra = (q * jnp.exp(d_q)).astype(BF16)
        k_intra = (kk * jnp.exp(-d_q)).astype(BF16)
        k_state = (kk * jnp.exp(d_k)).astype(BF16)
        q_inter = (q * e_g).astype(BF16)
        scores = [lax.dot_general(q_intra[sl], k_intra[sl], _NT, preferred_element_type=F32)
                  for sl in steps]
        upd_t = [lax.dot_general(v[sl], k_state[sl], _TN, preferred_element_type=F32)
                 for sl in steps]
        intra = [_dot(jnp.where(tril, sc, 0.0).astype(BF16), v[sl])
                 for sc, sl in zip(scores, steps)]
        return q_inter, e_g, upd_t, jnp.concatenate(intra, axis=0)

    def body(j, carry):
        rows = pl.ds(pl.multiple_of(j * sub, sub), sub)
        pre = [prepare(h, rows) for h in range(heads)]
        states = [state_ref[h] for h in range(heads)]
        inter = [[] for _ in range(heads)]
        for gi in range(sub // LANES):
            o_t = [None] * heads
            for m in range(per_grp):
                n = gi * per_grp + m
                for h in range(heads):
                    q_inter, e_g, upd_t, _ = pre[h]
                    res = lax.dot_general(states[h].astype(BF16),
                                          q_inter[gi * LANES:(gi + 1) * LANES], _NT,
                                          preferred_element_type=F32)
                    o_t[h] = res if o_t[h] is None else jnp.where(lane_step == m, res, o_t[h])
                    states[h] = states[h] * e_g[(n + 1) * step - 1:(n + 1) * step] + upd_t[n]
            for h in range(heads):
                inter[h].append(o_t[h].T)
        for h in range(heads):
            lanes = slice(h * LANES, (h + 1) * LANES)
            state_ref[h] = states[h]
            o = pre[h][3] + jnp.concatenate(inter[h], axis=0)
            o = o * lax.rsqrt(jnp.mean(o * o, axis=-1, keepdims=True) + RMS_EPS) * ng_all[:, lanes]
            o = o * jax.nn.silu(gate_ref[0, rows, lanes].astype(F32))
            o_ref[0, rows, lanes] = o.astype(BF16)
        return carry

    lax.fori_loop(0, q_ref.shape[1] // sub, body, 0, unroll=True)


def _hgrn2(proj3, lb_logits, norm_g, *, layer, ts=1024):
    b, s, _ = proj3.shape
    ts = min(ts, s)
    base = 3 * SB_WIDTH // HG_WIDTH
    depth1 = lb_logits.shape[0]

    def col(k):
        return pl.BlockSpec((1, ts, HG_WIDTH), lambda bi, si: (bi, si, base + k))

    return pl.pallas_call(
        functools.partial(_hg_kernel, layer=layer),
        grid=(b, s // ts),
        in_specs=[col(0), col(1), col(2), col(3),
                  pl.BlockSpec((depth1, HG_WIDTH), lambda bi, si: (0, 0)),
                  pl.BlockSpec((1, HG_WIDTH), lambda bi, si: (0, 0))],
        out_specs=pl.BlockSpec((1, ts, HG_WIDTH), lambda bi, si: (bi, si, 0)),
        out_shape=jax.ShapeDtypeStruct((b, s, HG_WIDTH), BF16),
        scratch_shapes=[pltpu.VMEM((HG_HEADS, HG_HEAD_DIM, HG_HEAD_DIM), F32)],
        compiler_params=pltpu.CompilerParams(dimension_semantics=("arbitrary", "arbitrary"),
                                             vmem_limit_bytes=VMEM_LIMIT),
        name="hgrn2",
    )(proj3, proj3, proj3, proj3, lb_logits, norm_g.reshape(1, HG_WIDTH))


def _proj_mlp_kernel(ma_ref, mb_ref, h_ref, wp_ref, bp_ref, g_ref, w1_ref, w2_ref, gf_ref,
                     o_ref, *, tf, final_norm):
    half = ma_ref.shape[1]
    h1 = (h_ref[...] + _dot(ma_ref[...], wp_ref[:half, :]) + _dot(mb_ref[...], wp_ref[half:, :])
          + bp_ref[...])
    u = _rms(h1, g_ref[...]).astype(BF16)
    acc = h1
    for c in range(w1_ref.shape[1] // tf):
        a = jnp.maximum(_dot(u, w1_ref[:, c * tf:(c + 1) * tf]), 0.0)
        acc = acc + _dot((a * a).astype(BF16), w2_ref[c * tf:(c + 1) * tf, :])
    if final_norm:
        acc = _rms(acc, gf_ref[...])
    o_ref[...] = acc


def _proj_mlp(ma, mb, cols_a, cols_b, h2d, wp, bp, g, w1, w2, gf, *, final_norm, tm=1024, tf=1024):
    t, d = h2d.shape
    half = d // 2
    dff = w1.shape[1]
    tm = min(tm, t)
    return pl.pallas_call(
        functools.partial(_proj_mlp_kernel, tf=tf, final_norm=final_norm),
        grid=(t // tm,),
        in_specs=[pl.BlockSpec((tm, half), lambda i: (i, cols_a)),
                  pl.BlockSpec((tm, half), lambda i: (i, cols_b)),
                  pl.BlockSpec((tm, d), lambda i: (i, 0)),
                  _const_spec((d, d)), _const_spec((1, d)), _const_spec((1, d)),
                  _const_spec((d, dff)), _const_spec((dff, d)), _const_spec((1, d))],
        out_specs=pl.BlockSpec((tm, d), lambda i: (i, 0)),
        out_shape=jax.ShapeDtypeStruct((t, d), F32),
        compiler_params=pltpu.CompilerParams(dimension_semantics=("arbitrary",),
                                             vmem_limit_bytes=VMEM_LIMIT),
        name="proj_mlp_final" if final_norm else "proj_mlp",
    )(ma, mb, h2d, wp, bp, g, w1, w2, gf)


def _conv_kernel(h_ref, g_ref, wg_ref, bg_ref, wdw_ref, bdw_ref, lng_ref, lnb_ref, o_ref,
                 abuf_ref, y_ref, *, rc):
    si = pl.program_id(1)
    ts, d = h_ref.shape[1], h_ref.shape[2]
    hist = CONV_HIST

    @pl.when(si == 0)
    def _():
        abuf_ref[0:hist, :] = jnp.zeros((hist, d), F32)

    @pl.when(si > 0)
    def _():
        abuf_ref[0:hist, :] = abuf_ref[ts:ts + hist, :]

    u = _rms(h_ref[0], g_ref[...]).astype(BF16)
    lin_a = _dot(u, wg_ref[:, :d]) + bg_ref[:, :d]
    lin_b = _dot(u, wg_ref[:, d:]) + bg_ref[:, d:]
    abuf_ref[hist:, :] = lin_a * jax.nn.sigmoid(lin_b)

    n_win = rc + hist
    pr = lax.broadcasted_iota(jnp.int32, ((SUBLANES - 1) * n_win, n_win), 0)
    pc = lax.broadcasted_iota(jnp.int32, ((SUBLANES - 1) * n_win, n_win), 1)
    shift_mat = jnp.where(pr % n_win - pc == pr // n_win + 1, 1.0, 0.0).astype(BF16)
    for l0 in range(0, d, 2 * LANES):
        for r0 in range(0, ts, rc):
            win2 = abuf_ref[r0:r0 + n_win, l0:l0 + 2 * LANES]
            shifted2 = _dot(shift_mat, win2.astype(BF16))
            for half in range(2):
                lanes = slice(l0 + half * LANES, l0 + (half + 1) * LANES)
                cols = slice(half * LANES, (half + 1) * LANES)
                acc = jnp.broadcast_to(bdw_ref[:, lanes], (rc, LANES))
                for sft in range(SUBLANES):
                    moved = (win2[:, cols] if sft == 0
                             else shifted2[(sft - 1) * n_win:sft * n_win, cols])
                    for dly in range(sft, CONV_WIDTH, SUBLANES):
                        tap = CONV_WIDTH - 1 - dly
                        start = hist - (dly - sft)
                        acc = acc + wdw_ref[tap:tap + 1, lanes] * moved[start:start + rc]
                y_ref[r0:r0 + rc, lanes] = acc

    y = y_ref[...]
    mu = jnp.mean(y, axis=-1, keepdims=True)
    yc = y - mu
    var = jnp.mean(yc * yc, axis=-1, keepdims=True)
    yn = yc * lax.rsqrt(var + LN_EPS) * lng_ref[...] + lnb_ref[...]
    o_ref[0] = (yn * jax.nn.sigmoid(yn)).astype(BF16)


def _conv_front(h3, g, wg, bg, wdw, bdw, lng, lnb, *, ts=1024, rc=64):
    b, s, d = h3.shape
    ts = min(ts, s)
    return pl.pallas_call(
        functools.partial(_conv_kernel, rc=rc),
        grid=(b, s // ts),
        in_specs=[pl.BlockSpec((1, ts, d), lambda bi, si: (bi, si, 0)),
                  _const_spec((1, d)), _const_spec((d, 2 * d)), _const_spec((1, 2 * d)),
                  _const_spec((CONV_WIDTH, d)), _const_spec((1, d)), _const_spec((1, d)),
                  _const_spec((1, d))],
        out_specs=pl.BlockSpec((1, ts, d), lambda bi, si: (bi, si, 0)),
        out_shape=jax.ShapeDtypeStruct((b, s, d), BF16),
        scratch_shapes=[pltpu.VMEM((ts + CONV_HIST, d), F32), pltpu.VMEM((ts, d), F32)],
        compiler_params=pltpu.CompilerParams(dimension_semantics=("arbitrary", "arbitrary"),
                                             vmem_limit_bytes=VMEM_LIMIT),
        name="conv_front",
    )(h3, g, wg, bg, wdw, bdw, lng, lnb)


def kernel(x, norm_mix_g, norm_ffn_g, w_in_ab, w_out_ab, hg_lb_logits, hg_norm_g, conv_w_glu,
           conv_b_glu, conv_w_dw, conv_b_dw, conv_ln_g, conv_ln_b, conv_w_pw, conv_b_pw,
           w_ff1, w_ff2, final_norm_g):
    b, s, d = x.shape
    depth = norm_mix_g.shape[0]
    row = lambda a: a.reshape(1, -1).astype(F32)
    zeros_d = jnp.zeros((1, d), F32)
    h = x.reshape(b * s, d)
    for layer in range(depth):
        j = layer // 2
        last = layer == depth - 1
        if layer % 2 == 0:
            q_scale = SB_HEAD_DIM ** -0.5 * LOG2E
            scale = jnp.where(jnp.arange(w_in_ab.shape[2]) < SB_WIDTH, q_scale, 1.0)
            w_in = (w_in_ab[j] * scale).astype(BF16)
            proj = _inproj(h, row(norm_mix_g[layer]), w_in).reshape(b, s, -1)
            o_sb = _sb_attention(proj).reshape(b * s, SB_WIDTH)
            o_hg = _hgrn2(proj, hg_lb_logits.astype(F32), hg_norm_g[j].astype(F32),
                          layer=layer).reshape(b * s, HG_WIDTH)
            mix = (o_sb, o_hg, 0, 0)
            wp, bp = w_out_ab[j].astype(BF16), zeros_d
        else:
            y = _conv_front(h.reshape(b, s, d), row(norm_mix_g[layer]),
                            conv_w_glu[j].astype(BF16), row(conv_b_glu[j]),
                            conv_w_dw[j].astype(F32), row(conv_b_dw[j]), row(conv_ln_g[j]),
                            row(conv_ln_b[j])).reshape(b * s, d)
            mix = (y, y, 0, 1)
            wp, bp = conv_w_pw[j].astype(BF16), row(conv_b_pw[j])
        h = _proj_mlp(*mix, h, wp, bp, row(norm_ffn_g[layer]), w_ff1[layer].astype(BF16),
                      w_ff2[layer].astype(BF16), row(final_norm_g), final_norm=last)
    return h.reshape(b, s, d)
```
